```python
import math, functools
import jax, jax.numpy as jnp
from jax import lax
import numpy as np

D_MODEL = 2048
BATCH = 2
SEQ = 4096
DEPTH = 2
DEC_BATCH = 128
DEC_SEQ = 4
PAST_LEN = 2048
PAGE_SIZE = 128

HEAD_DIM = 128
A_HEADS = 8
A_KV = 2
A_WIDTH = A_HEADS * HEAD_DIM
IDX_HEADS = 16
IDX_DIM = 64
B_HEADS = 4
B_KV = 2
B_VDIM = 2 * HEAD_DIM
B_WIDTH = B_HEADS * B_VDIM
N_META = 16
TOP_K_MAX = 256
Q_BLOCK = 128
ROPE_THETA = 10000.0
EPS = 1e-6
SPLIT_SIZES = (A_WIDTH, A_KV * HEAD_DIM, A_KV * HEAD_DIM, A_WIDTH,
               IDX_HEADS * IDX_DIM, IDX_DIM, IDX_HEADS,
               B_HEADS * 2 * HEAD_DIM, B_KV * 2 * HEAD_DIM, B_KV * B_VDIM, B_WIDTH,
               D_MODEL, D_MODEL)
N_IN = sum(SPLIT_SIZES)
SPLIT_POINTS = tuple(int(v) for v in np.cumsum(SPLIT_SIZES)[:-1])

kernel_name = 'meta_dsa_diffattn_gated_hybrid_step'


def lambda_init(layer):
    return 0.8 - 0.6 * math.exp(-0.3 * layer)


def rmsnorm(x, g):
    xf = x.astype(jnp.float32)
    y = xf * lax.rsqrt(jnp.mean(xf * xf, axis=-1, keepdims=True) + EPS)
    return (y * g.astype(jnp.float32)).astype(x.dtype)


def rope(x, pos):
    half = x.shape[-1] // 2
    inv = ROPE_THETA ** (-jnp.arange(half, dtype=jnp.float32) / half)
    ang = pos.astype(jnp.float32)[:, None] * inv[None, :]
    shape = (1, pos.shape[0]) + (1,) * (x.ndim - 3) + (half,)
    cos, sin = jnp.cos(ang).reshape(shape), jnp.sin(ang).reshape(shape)
    x1 = x[..., :half].astype(jnp.float32)
    x2 = x[..., half:].astype(jnp.float32)
    return jnp.concatenate([x1 * cos - x2 * sin, x2 * cos + x1 * sin], axis=-1).astype(x.dtype)


def project(h, pos, w_in):
    B, T, _ = h.shape
    (q_a, k_a, v_a, z_a, qi, ki, wi, q_b, k_b, v_b, z_b, g_a, g_b) = jnp.split(h @ w_in, SPLIT_POINTS, axis=-1)
    q_a = rope(q_a.reshape(B, T, A_HEADS, HEAD_DIM), pos)
    k_a = rope(k_a.reshape(B, T, A_KV, HEAD_DIM), pos)
    v_a = v_a.reshape(B, T, A_KV, HEAD_DIM)
    qi = rope(qi.reshape(B, T, IDX_HEADS, IDX_DIM), pos)
    ki = rope(ki.reshape(B, T, 1, IDX_DIM), pos).reshape(B, T, IDX_DIM)
    wi = wi * (IDX_HEADS ** -0.5)
    q_b = rope(q_b.reshape(B, T, B_HEADS, 2, HEAD_DIM), pos)
    k_b = rope(k_b.reshape(B, T, B_KV, 2, HEAD_DIM), pos)
    v_b = v_b.reshape(B, T, B_KV, B_VDIM)
    return q_a, k_a, v_a, z_a, qi, ki, wi, q_b, k_b, v_b, z_b, g_a, g_b


def dsa_attend(q, qi, wi, pos, k_idx, k, v, k_top):
    B, T = q.shape[:2]
    S = k.shape[1]
    dots = jnp.einsum('btjd,bsd->btjs', qi, k_idx).astype(jnp.float32) * (IDX_DIM ** -0.5)
    score = jnp.einsum('btjs,btj->bts', jax.nn.relu(dots), wi.astype(jnp.float32))
    causal = jnp.arange(S)[None, :] <= pos[:, None]
    score = jnp.where(causal[None], score, -jnp.inf)
    top_val, top_idx = lax.top_k(score, k_top)
    valid = jnp.isfinite(top_val)
    gather = jax.vmap(lambda a, i: a[i])
    k_sel = gather(k, top_idx)
    v_sel = gather(v, top_idx)
    qg = q.reshape(B, T, A_KV, A_HEADS // A_KV, HEAD_DIM)
    s = jnp.einsum('btgrd,btkgd->btgrk', qg, k_sel).astype(jnp.float32) * (HEAD_DIM ** -0.5)
    s = jnp.where(valid[:, :, None, None, :], s, -jnp.inf)
    p = jax.nn.softmax(s, axis=-1).astype(v.dtype)
    o = jnp.einsum('btgrk,btkgd->btgrd', p, v_sel)
    return o.reshape(B, T, A_WIDTH)


def diff_attend(q, pos, k, v, lam, lam_init, g_sub):
    B, T = q.shape[:2]
    S = k.shape[1]
    qg = q.reshape(B, T, B_KV, B_HEADS // B_KV, 2, HEAD_DIM)
    s = jnp.einsum('btgrcd,bsgcd->bgrcts', qg, k).astype(jnp.float32) * (HEAD_DIM ** -0.5)
    causal = jnp.arange(S)[None, :] <= pos[:, None]
    s = jnp.where(causal, s, -jnp.inf)
    p = jax.nn.softmax(s, axis=-1)
    a = (p[:, :, :, 0] - lam * p[:, :, :, 1]).astype(v.dtype)
    o = jnp.einsum('bgrts,bsgd->btgrd', a, v)
    o = rmsnorm(o, g_sub) * (1.0 - lam_init)
    return o.reshape(B, T, B_WIDTH)


def sweep_query_blocks(fn, q_arrays, pos):
    B, T = q_arrays[0].shape[:2]
    nb = T // Q_BLOCK
    qb = tuple(jnp.swapaxes(a.reshape((B, nb, Q_BLOCK) + a.shape[2:]), 0, 1) for a in q_arrays)
    out = lax.map(lambda args: fn(*args), qb + (pos.reshape(nb, Q_BLOCK),))
    return jnp.swapaxes(out, 0, 1).reshape(B, T, -1)


def merge(x, o_a, o_b, z_a, z_b, g_a, g_b, w_pa, w_pb, w_o):
    branch_a = (o_a * jax.nn.silu(z_a)) @ w_pa
    branch_b = (o_b * jax.nn.silu(z_b)) @ w_pb
    m = jax.nn.sigmoid(g_a) * branch_a + jax.nn.sigmoid(g_b) * branch_b
    return x + m @ w_o


def gather_pages(pool, page_table):
    g = pool[page_table]
    return g.reshape((page_table.shape[0], page_table.shape[1] * pool.shape[1]) + pool.shape[2:])


def setup_inputs(seed: int = 0) -> dict:
    key = jax.random.key(seed)
    ks = jax.random.split(key, 20)
    n_pages = PAST_LEN // PAGE_SIZE
    n_used = DEC_BATCH * n_pages
    n_pool = n_used + max(1, n_used // 4)
    f32 = jnp.float32
    nrm = lambda k, shape, s=1.0: jax.random.normal(k, shape, f32) * s
    page_table = jax.random.permutation(ks[7], n_pool)[:n_used].reshape(DEC_BATCH, n_pages).astype(jnp.int32)
    return {
        'x_prompt': nrm(ks[0], (BATCH, SEQ, D_MODEL)),
        'x_sample': nrm(ks[1], (DEC_BATCH, DEC_SEQ, D_MODEL)),
        'cache_k_a': nrm(ks[2], (DEPTH, n_pool, PAGE_SIZE, A_KV, HEAD_DIM)),
        'cache_v_a': nrm(ks[3], (DEPTH, n_pool, PAGE_SIZE, A_KV, HEAD_DIM)),
        'cache_k_idx': nrm(ks[4], (DEPTH, n_pool, PAGE_SIZE, IDX_DIM)),
        'cache_k_b': nrm(ks[5], (DEPTH, n_pool, PAGE_SIZE, B_KV, 2, HEAD_DIM)),
        'cache_v_b': nrm(ks[6], (DEPTH, n_pool, PAGE_SIZE, B_KV, B_VDIM)),
        'page_table': page_table,
        'meta_tokens': nrm(ks[8], (N_META, D_MODEL)),
        'norm_g': 1.0 + nrm(ks[9], (DEPTH, D_MODEL), 0.02),
        'w_in': nrm(ks[10], (DEPTH, D_MODEL, N_IN), D_MODEL ** -0.5),
        'lam_qk': nrm(ks[11], (DEPTH, 4, HEAD_DIM), 0.1),
        'subln_g': 1.0 + nrm(ks[12], (DEPTH, B_VDIM), 0.02),
        'w_proj_a': nrm(ks[13], (DEPTH, A_WIDTH, D_MODEL), A_WIDTH ** -0.5),
        'w_proj_b': nrm(ks[14], (DEPTH, B_WIDTH, D_MODEL), B_WIDTH ** -0.5),
        'w_out': nrm(ks[15], (DEPTH, D_MODEL, D_MODEL), D_MODEL ** -0.5),
        'final_norm_g': 1.0 + nrm(ks[16], (D_MODEL,), 0.02),
    }


def reference(x_prompt, x_sample, cache_k_a, cache_v_a, cache_k_idx, cache_k_b, cache_v_b, page_table,
              meta_tokens, norm_g, w_in, lam_qk, subln_g, w_proj_a, w_proj_b, w_out, final_norm_g):
    L = N_META + SEQ
    L_pad = -(-L // Q_BLOCK) * Q_BLOCK
    meta = jnp.broadcast_to(meta_tokens.astype(x_prompt.dtype)[None], (BATCH, N_META, D_MODEL))
    xp = jnp.concatenate([meta, x_prompt], axis=1)
    xp = jnp.pad(xp, ((0, 0), (0, L_pad - L), (0, 0)))
    pos_p = jnp.arange(L_pad, dtype=jnp.int32)
    xs = x_sample
    pos_s = PAST_LEN + jnp.arange(DEC_SEQ, dtype=jnp.int32)
    k_top_p = min(TOP_K_MAX, SEQ // 4)
    k_top_s = min(TOP_K_MAX, (PAST_LEN + DEC_SEQ) // 4)
    rows_p = [[] for _ in range(5)]
    rows_s = [[] for _ in range(5)]
    for l in range(DEPTH):
        lam_init = lambda_init(l)
        lq = lam_qk[l].astype(jnp.float32)
        lam = jnp.exp(jnp.sum(lq[0] * lq[1])) - jnp.exp(jnp.sum(lq[2] * lq[3])) + lam_init

        h = rmsnorm(xp, norm_g[l])
        q_a, k_a, v_a, z_a, qi, ki, wi, q_b, k_b, v_b, z_b, g_a, g_b = project(h, pos_p, w_in[l])
        o_a = sweep_query_blocks(functools.partial(dsa_attend, k_idx=ki, k=k_a, v=v_a, k_top=k_top_p),
                                 (q_a, qi, wi), pos_p)
        o_b = sweep_query_blocks(functools.partial(diff_attend, k=k_b, v=v_b, lam=lam, lam_init=lam_init,
                                                   g_sub=subln_g[l]), (q_b,), pos_p)
        xp = merge(xp, o_a, o_b, z_a, z_b, g_a, g_b, w_proj_a[l], w_proj_b[l], w_out[l])
        for i, a in enumerate((k_a, v_a, ki, k_b, v_b)):
            rows_p[i].append(a[:, :L])

        h = rmsnorm(xs, norm_g[l])
        q_a, k_a, v_a, z_a, qi, ki, wi, q_b, k_b, v_b, z_b, g_a, g_b = project(h, pos_s, w_in[l])
        ka_all = jnp.concatenate([gather_pages(cache_k_a[l], page_table), k_a], axis=1)
        va_all = jnp.concatenate([gather_pages(cache_v_a[l], page_table), v_a], axis=1)
        ki_all = jnp.concatenate([gather_pages(cache_k_idx[l], page_table), ki], axis=1)
        kb_all = jnp.concatenate([gather_pages(cache_k_b[l], page_table), k_b], axis=1)
        vb_all = jnp.concatenate([gather_pages(cache_v_b[l], page_table), v_b], axis=1)
        o_a = dsa_attend(q_a, qi, wi, pos_s, ki_all, ka_all, va_all, k_top_s)
        o_b = diff_attend(q_b, pos_s, kb_all, vb_all, lam, lam_init, subln_g[l])
        xs = merge(xs, o_a, o_b, z_a, z_b, g_a, g_b, w_proj_a[l], w_proj_b[l], w_out[l])
        for i, a in enumerate((k_a, v_a, ki, k_b, v_b)):
            rows_s[i].append(a)

    y_prompt = rmsnorm(xp, final_norm_g)[:, N_META:L]
    y_sample = rmsnorm(xs, final_norm_g)
    new_k_a_p = jnp.stack(rows_p[0])
    new_v_a_p = jnp.stack(rows_p[1])
    new_k_idx_p = jnp.stack(rows_p[2])
    new_k_b_p = jnp.stack(rows_p[3])
    new_v_b_p = jnp.stack(rows_p[4])
    new_k_a_s = jnp.stack(rows_s[0])
    new_v_a_s = jnp.stack(rows_s[1])
    new_k_idx_s = jnp.stack(rows_s[2])
    new_k_b_s = jnp.stack(rows_s[3])
    new_v_b_s = jnp.stack(rows_s[4])
    return (y_prompt, y_sample, new_k_a_p, new_v_a_p, new_k_idx_p, new_k_b_p, new_v_b_p,
            new_k_a_s, new_v_a_s, new_k_idx_s, new_k_b_s, new_v_b_s)
```

```python
import functools
import math

import jax
import jax.numpy as jnp
import numpy as np
from jax import lax
from jax.experimental import pallas as pl
from jax.experimental.pallas import tpu as pltpu

HEAD_DIM = 128
A_HEADS = 8
A_KV = 2
A_GROUP = A_HEADS // A_KV
A_WIDTH = A_HEADS * HEAD_DIM
IDX_HEADS = 16
IDX_DIM = 64
B_HEADS = 4
B_KV = 2
B_GROUP = B_HEADS // B_KV
B_VDIM = 2 * HEAD_DIM
B_WIDTH = B_HEADS * B_VDIM
TOP_K_MAX = 256
ROPE_THETA = 10000.0
EPS = 1e-6

LANES = 128
ROW_TILE = 256
KEY_CHUNK = 256
DSA_Q_TILE = 128
DIFF_Q_TILE = 256
MERGE_TILE = 256
DEC_ROWS = 8
VMEM_LIMIT = 48 * 1024 * 1024

NEG_BIG = -1e30
INT_MIN = -2 ** 31
KEY_NEG_INF = -2139095041
KEY_POS_INF = 2139095040

F32 = jnp.float32
BF16 = jnp.bfloat16
I32 = jnp.int32

_NT = (((1,), (1,)), ((), ()))


def _params(*sem):
    return pltpu.CompilerParams(dimension_semantics=sem, vmem_limit_bytes=VMEM_LIMIT)


def _lambda_init(layer):
    return 0.8 - 0.6 * math.exp(-0.3 * layer)


def _float_key(x):
    b = lax.bitcast_convert_type(x, I32)
    return b ^ ((b >> 31) & jnp.int32(0x7FFFFFFF))


def _rmsnorm_kernel(x_ref, g_ref, o_ref):
    x = x_ref[...]
    ms = jnp.mean(x * x, axis=-1, keepdims=True)
    o_ref[...] = (x * lax.rsqrt(ms + EPS) * g_ref[...]).astype(o_ref.dtype)


def _rmsnorm(x, g, out_dtype):
    m, d = x.shape
    tm = min(ROW_TILE, m)
    return pl.pallas_call(
        _rmsnorm_kernel,
        grid=(m // tm,),
        in_specs=[pl.BlockSpec((tm, d), lambda i: (i, 0)),
                  pl.BlockSpec((1, d), lambda i: (0, 0))],
        out_specs=pl.BlockSpec((tm, d), lambda i: (i, 0)),
        out_shape=jax.ShapeDtypeStruct((m, d), out_dtype),
        compiler_params=_params("arbitrary"),
        name="rmsnorm",
    )(x, g.reshape(1, d).astype(F32))


def _rope128(a, c, s):
    return a * c + pltpu.roll(a, HEAD_DIM // 2, 1) * s


def _rope64(a, c, s_lo, s_hi):
    return a * c + pltpu.roll(a, LANES - IDX_DIM // 2, 1) * s_lo + pltpu.roll(a, IDX_DIM // 2, 1) * s_hi


def _proj_q_kernel(h_ref, w_ref, c_ref, s_ref, o_ref):
    acc = jnp.dot(h_ref[...], w_ref[...], preferred_element_type=F32)
    c, s = c_ref[...], s_ref[...]
    for j in range(acc.shape[1] // LANES):
        sl = slice(j * LANES, (j + 1) * LANES)
        o_ref[:, sl] = _rope128(acc[:, sl], c, s).astype(o_ref.dtype)


def _proj_kv_kernel(h_ref, w_ref, c_ref, s_ref, kv_ref, kb_ref, vb_ref, ka_ref, va_ref):
    acc = jnp.dot(h_ref[0], w_ref[...], preferred_element_type=F32)
    c, s = c_ref[...], s_ref[...]
    outs = ((kb_ref, 0, 4, True), (vb_ref, 4, 4, False), (ka_ref, 8, 2, True), (va_ref, 10, 2, False))
    for ref, start, count, rot in outs:
        for j in range(count):
            a = acc[:, (start + j) * LANES:(start + j + 1) * LANES]
            if rot:
                a = _rope128(a, c, s)
            ref[0, :, j * LANES:(j + 1) * LANES] = a
            kv_ref[0, :, (start + j) * LANES:(start + j + 1) * LANES] = a.astype(kv_ref.dtype)


def _proj_idx_kernel(h_ref, w_ref, c_ref, slo_ref, shi_ref, ck_ref, klo_ref, khi_ref,
                     q_ref, kw_ref, kk_ref, *, expand):
    acc = jnp.dot(h_ref[...], w_ref[...], preferred_element_type=F32)
    c, slo, shi = c_ref[...], slo_ref[...], shi_ref[...]
    lane = lax.broadcasted_iota(I32, (acc.shape[0], LANES), 1)
    low = lane < IDX_DIM
    for j in range(IDX_HEADS // 2):
        a = _rope64(acc[:, j * LANES:(j + 1) * LANES], c, slo, shi)
        if expand:
            q_ref[:, (2 * j) * LANES:(2 * j + 1) * LANES] = jnp.where(low, a, 0.0).astype(q_ref.dtype)
            q_ref[:, (2 * j + 1) * LANES:(2 * j + 2) * LANES] = jnp.where(low, 0.0, a).astype(q_ref.dtype)
        else:
            q_ref[:, j * LANES:(j + 1) * LANES] = a.astype(q_ref.dtype)
    kw = _rope64(acc[:, IDX_HEADS * IDX_DIM:], ck_ref[...], klo_ref[...], khi_ref[...])
    kw_ref[...] = kw
    kk_ref[...] = jnp.where(low, kw, pltpu.roll(kw, IDX_DIM, 1)).astype(kk_ref.dtype)


def _proj_plain_kernel(h_ref, w_ref, o_ref):
    o_ref[...] = jnp.dot(h_ref[...], w_ref[...], preferred_element_type=F32).astype(o_ref.dtype)


def _row_spec(tm, width, ntab=None):
    if ntab is None:
        return pl.BlockSpec((tm, width), lambda j, i: (i, 0))
    return pl.BlockSpec((tm, width), lambda j, i: (i % ntab, 0))


def _proj_q(h, w, tabs):
    m, d = h.shape
    n = w.shape[1]
    tm = min(ROW_TILE, m)
    ntab = tabs[0].shape[0] // tm
    return pl.pallas_call(
        _proj_q_kernel,
        grid=(1, m // tm),
        in_specs=[_row_spec(tm, d), pl.BlockSpec((d, n), lambda j, i: (0, 0)),
                  _row_spec(tm, LANES, ntab), _row_spec(tm, LANES, ntab)],
        out_specs=_row_spec(tm, n),
        out_shape=jax.ShapeDtypeStruct((m, n), BF16),
        compiler_params=_params("arbitrary", "arbitrary"),
        name="proj_q",
    )(h, w, tabs[0], tabs[1])


def _proj_kv(h3, w, tabs, n_keep):
    nb, lp, d = h3.shape
    n = w.shape[1]
    tm = min(DIFF_Q_TILE, lp)
    row = lambda width: pl.BlockSpec((1, tm, width), lambda b, i: (b, i, 0))
    tab = pl.BlockSpec((tm, LANES), lambda b, i: (i, 0))
    f32_out = lambda width: jax.ShapeDtypeStruct((nb, n_keep, width), F32)
    return pl.pallas_call(
        _proj_kv_kernel,
        grid=(nb, lp // tm),
        in_specs=[row(d), pl.BlockSpec((d, n), lambda b, i: (0, 0)), tab, tab],
        out_specs=[row(n), row(4 * LANES), row(4 * LANES), row(2 * LANES), row(2 * LANES)],
        out_shape=[jax.ShapeDtypeStruct((nb, lp, n), BF16),
                   f32_out(4 * LANES), f32_out(4 * LANES), f32_out(2 * LANES), f32_out(2 * LANES)],
        compiler_params=_params("arbitrary", "arbitrary"),
        name="proj_kv",
    )(h3, w, tabs[0], tabs[1])


def _proj_idx(h, w, tabs, expand):
    m, d = h.shape
    n = w.shape[1]
    tm = min(ROW_TILE, m)
    ntab = tabs[0].shape[0] // tm
    qw = IDX_HEADS * (LANES if expand else IDX_DIM)
    return pl.pallas_call(
        functools.partial(_proj_idx_kernel, expand=expand),
        grid=(1, m // tm),
        in_specs=[_row_spec(tm, d), pl.BlockSpec((d, n), lambda j, i: (0, 0))]
                 + [_row_spec(tm, LANES, ntab)] * 6,
        out_specs=[_row_spec(tm, qw), _row_spec(tm, LANES), _row_spec(tm, LANES)],
        out_shape=[jax.ShapeDtypeStruct((m, qw), BF16),
                   jax.ShapeDtypeStruct((m, LANES), F32),
                   jax.ShapeDtypeStruct((m, LANES), BF16)],
        compiler_params=_params("arbitrary", "arbitrary"),
        name="proj_idx",
    )(h, w, *tabs)


def _proj_plain(h, w, tn):
    m, d = h.shape
    n = w.shape[1]
    tm = min(ROW_TILE, m)
    return pl.pallas_call(
        _proj_plain_kernel,
        grid=(n // tn, m // tm),
        in_specs=[_row_spec(tm, d), pl.BlockSpec((d, tn), lambda j, i: (0, j))],
        out_specs=pl.BlockSpec((tm, tn), lambda j, i: (i, j)),
        out_shape=jax.ShapeDtypeStruct((m, n), F32),
        compiler_params=_params("arbitrary", "arbitrary"),
        name="proj_gate",
    )(h, w)


def _search_threshold(count_ge, rows, k_top):
    def bit_body(bi, t_u):
        bit = lax.shift_left(jnp.int32(1), 31 - bi)
        cand_u = t_u | bit
        cnt = count_ge(cand_u ^ jnp.int32(INT_MIN))
        return jnp.where(cnt >= k_top, cand_u, t_u)
    t_u = lax.fori_loop(0, 32, bit_body, jnp.zeros((rows, 1), I32))
    return t_u ^ jnp.int32(INT_MIN)


def _strict_upper(n):
    r = lax.broadcasted_iota(I32, (n, n), 0)
    c = lax.broadcasted_iota(I32, (n, n), 1)
    return jnp.where(r < c, 1.0, 0.0).astype(BF16)


def _dsa_kernel(qa_ref, qe_ref, kw_ref, kk_ref, ka_ref, va_ref, o_ref, sk_ref, *, k_top):
    tq = qa_ref.shape[0]
    kc = KEY_CHUNK
    i = pl.program_id(1)
    q0 = i * tq
    nkc = (q0 + tq + kc - 1) // kc
    row_pos = q0 + lax.broadcasted_iota(I32, (tq, kc), 0)
    col_iota = lax.broadcasted_iota(I32, (tq, kc), 1)

    w_idx = kw_ref[:, IDX_DIM:IDX_DIM + IDX_HEADS] * (IDX_DIM ** -0.5)
    w_cols = [jnp.broadcast_to(w_idx[:, h:h + 1], (tq, kc)) for h in range(IDX_HEADS)]
    q_stack = jnp.concatenate([qe_ref[:, h * LANES:(h + 1) * LANES] for h in range(IDX_HEADS)], axis=0)

    def score_chunk(c):
        k0 = pl.multiple_of(c * kc, kc)
        d = lax.dot_general(q_stack, kk_ref[pl.ds(k0, kc), :], _NT, preferred_element_type=F32)
        acc = jnp.zeros((tq, kc), F32)
        for h in range(IDX_HEADS):
            acc = acc + jnp.maximum(d[h * tq:(h + 1) * tq], 0.0) * w_cols[h]
        return _float_key(acc), k0

    def p1_body(c, carry):
        key, _ = score_chunk(c)
        sk_ref[c] = key
        return carry
    lax.fori_loop(0, nkc - 1, p1_body, 0)
    key, k0 = score_chunk(nkc - 1)
    sk_ref[nkc - 1] = jnp.where(k0 + col_iota <= row_pos, key, jnp.int32(INT_MIN))

    def count_cmp(cand, strict):
        cand_b = jnp.broadcast_to(cand, (tq, LANES))
        def body(c, acc):
            k = sk_ref[c]
            for j in range(kc // LANES):
                kj = k[:, j * LANES:(j + 1) * LANES]
                hit = (kj > cand_b) if strict else (kj >= cand_b)
                acc = acc + jnp.where(hit, 1.0, 0.0)
            return acc
        acc = lax.fori_loop(0, nkc, body, jnp.zeros((tq, LANES), F32))
        return jnp.sum(acc, axis=1, keepdims=True)

    thr = _search_threshold(lambda cand: count_cmp(cand, False), tq, float(k_top))
    c_ge = count_cmp(thr, False)
    c_gt = count_cmp(thr, True)
    need = float(k_top) - c_gt
    tie_rows = jnp.where((c_ge > float(k_top)) & (thr > jnp.int32(INT_MIN)), 1.0, 0.0)
    has_ties = jnp.max(tie_rows) > 0.0

    @pl.when(has_ties)
    def _():
        tri = _strict_upper(kc)
        def body(c, run):
            k = sk_ref[c]
            eq = k == thr
            eqf = jnp.where(eq, 1.0, 0.0)
            before = jnp.dot(eqf.astype(BF16), tri, preferred_element_type=F32) + run
            sk_ref[c] = jnp.where(eq & (before >= need), jnp.int32(INT_MIN), k)
            return run + jnp.sum(eqf, axis=1, keepdims=True)
        lax.fori_loop(0, nkc, body, jnp.zeros((tq, 1), F32))

    scale = HEAD_DIM ** -0.5
    for g in range(A_KV):
        qg = jnp.concatenate([qa_ref[:, (g * A_GROUP + r) * HEAD_DIM:(g * A_GROUP + r + 1) * HEAD_DIM]
                              for r in range(A_GROUP)], axis=0)

        def body(c, carry, g=g, qg=qg):
            k0 = pl.multiple_of(c * kc, kc)
            kch = ka_ref[0, pl.ds(k0, kc), g * HEAD_DIM:(g + 1) * HEAD_DIM]
            vch = va_ref[0, pl.ds(k0, kc), g * HEAD_DIM:(g + 1) * HEAD_DIM]
            s = lax.dot_general(qg, kch, _NT, preferred_element_type=F32) * scale
            key = sk_ref[c]
            sel = (key >= thr) & (key > jnp.int32(KEY_NEG_INF)) & (key < jnp.int32(KEY_POS_INF))
            ps, new = [], []
            for r in range(A_GROUP):
                m, l, _ = carry[r]
                sr = jnp.where(sel, s[r * tq:(r + 1) * tq], NEG_BIG)
                m_new = jnp.maximum(m, jnp.max(sr, axis=1, keepdims=True))
                p = jnp.exp(sr - m_new)
                alpha = jnp.exp(m - m_new)
                new.append((m_new, alpha * l + jnp.sum(p, axis=1, keepdims=True), alpha))
                ps.append(p.astype(BF16))
            pv = jnp.dot(jnp.concatenate(ps, axis=0), vch, preferred_element_type=F32)
            return tuple((new[r][0], new[r][1], new[r][2] * carry[r][2] + pv[r * tq:(r + 1) * tq])
                         for r in range(A_GROUP))

        init = tuple((jnp.full((tq, 1), NEG_BIG, F32), jnp.zeros((tq, 1), F32),
                      jnp.zeros((tq, HEAD_DIM), F32)) for _ in range(A_GROUP))
        fin = lax.fori_loop(0, nkc, body, init)
        for r in range(A_GROUP):
            hh = g * A_GROUP + r
            o_ref[:, hh * HEAD_DIM:(hh + 1) * HEAD_DIM] = fin[r][2] / fin[r][1]


def _dsa_prompt(qrot, qe, kw, kk, kvb, nb, lp, k_top):
    m = qrot.shape[0]
    tq = DSA_Q_TILE
    nq = lp // tq
    qrow = lambda width: pl.BlockSpec((tq, width), lambda b, i: (b * nq + i, 0))
    return pl.pallas_call(
        functools.partial(_dsa_kernel, k_top=k_top),
        grid=(nb, nq),
        in_specs=[qrow(A_WIDTH), qrow(IDX_HEADS * LANES), qrow(LANES),
                  pl.BlockSpec((lp, LANES), lambda b, i: (b, 0)),
                  pl.BlockSpec((1, lp, A_KV * HEAD_DIM), lambda b, i: (b, 0, 4)),
                  pl.BlockSpec((1, lp, A_KV * HEAD_DIM), lambda b, i: (b, 0, 5))],
        out_specs=qrow(A_WIDTH),
        out_shape=jax.ShapeDtypeStruct((m, A_WIDTH), F32),
        scratch_shapes=[pltpu.VMEM((lp // KEY_CHUNK, tq, KEY_CHUNK), I32)],
        compiler_params=_params("arbitrary", "arbitrary"),
        name="dsa_prompt",
    )(qrot, qe, kw, kk, kvb, kvb)


def _lambda_full(lq_ref, lam_init):
    lq = lq_ref[...]
    a = jnp.sum(lq[0:1] * lq[1:2], axis=1, keepdims=True)
    b = jnp.sum(lq[2:3] * lq[3:4], axis=1, keepdims=True)
    return jnp.exp(a) - jnp.exp(b) + lam_init


def _subln(o, g, lam_init):
    ms = jnp.mean(o * o, axis=-1, keepdims=True)
    return (o * lax.rsqrt(ms + EPS) * g) * (1.0 - lam_init)


def _diff_kernel(qb_ref, kb_ref, vb_ref, lq_ref, g_ref, o_ref, *, lam_init):
    tq = qb_ref.shape[0]
    kc = KEY_CHUNK
    i = pl.program_id(1)
    q0 = i * tq
    n_full = q0 // kc
    n_all = (q0 + tq + kc - 1) // kc
    lam = _lambda_full(lq_ref, lam_init)
    scale = HEAD_DIM ** -0.5
    rows = B_GROUP * tq
    row_pos = q0 + lax.broadcasted_iota(I32, (tq, kc), 0)
    col_iota = lax.broadcasted_iota(I32, (tq, kc), 1)
    row_pos = jnp.concatenate([row_pos] * B_GROUP, axis=0)
    col_iota = jnp.concatenate([col_iota] * B_GROUP, axis=0)

    for g in range(B_KV):
        outs = []
        for c2 in range(2):
            qgc = jnp.concatenate(
                [qb_ref[:, ((g * B_GROUP + r) * 2 + c2) * HEAD_DIM:((g * B_GROUP + r) * 2 + c2 + 1) * HEAD_DIM]
                 for r in range(B_GROUP)], axis=0)
            kcol = (g * 2 + c2) * HEAD_DIM

            def step(c, carry, masked, qgc=qgc, kcol=kcol, g=g):
                m, l, acc = carry
                k0 = pl.multiple_of(c * kc, kc)
                kch = kb_ref[0, pl.ds(k0, kc), kcol:kcol + HEAD_DIM]
                vch = vb_ref[0, pl.ds(k0, kc), g * B_VDIM:(g + 1) * B_VDIM]
                s = lax.dot_general(qgc, kch, _NT, preferred_element_type=F32) * scale
                if masked:
                    s = jnp.where(k0 + col_iota <= row_pos, s, NEG_BIG)
                m_new = jnp.maximum(m, jnp.max(s, axis=1, keepdims=True))
                p = jnp.exp(s - m_new)
                alpha = jnp.exp(m - m_new)
                l = alpha * l + jnp.sum(p, axis=1, keepdims=True)
                acc = alpha * acc + jnp.dot(p.astype(BF16), vch, preferred_element_type=F32)
                return m_new, l, acc

            init = (jnp.full((rows, 1), NEG_BIG, F32), jnp.zeros((rows, 1), F32),
                    jnp.zeros((rows, B_VDIM), F32))
            carry = lax.fori_loop(0, n_full, functools.partial(step, masked=False), init)
            carry = lax.fori_loop(n_full, n_all, functools.partial(step, masked=True), carry)
            outs.append(carry[2] / carry[1])
        o = _subln(outs[0] - lam * outs[1], g_ref[...], lam_init)
        for r in range(B_GROUP):
            hh = g * B_GROUP + r
            o_ref[:, hh * B_VDIM:(hh + 1) * B_VDIM] = o[r * tq:(r + 1) * tq]


def _diff_prompt(qrot, kvb, lam_qk, g_sub, nb, lp, lam_init):
    m = qrot.shape[0]
    tq = DIFF_Q_TILE
    nq = lp // tq
    return pl.pallas_call(
        functools.partial(_diff_kernel, lam_init=lam_init),
        grid=(nb, nq),
        in_specs=[pl.BlockSpec((tq, B_HEADS * 2 * HEAD_DIM), lambda b, i: (b * nq + i, 1)),
                  pl.BlockSpec((1, lp, B_KV * 2 * HEAD_DIM), lambda b, i: (b, 0, 0)),
                  pl.BlockSpec((1, lp, B_KV * B_VDIM), lambda b, i: (b, 0, 1)),
                  pl.BlockSpec((4, HEAD_DIM), lambda b, i: (0, 0)),
                  pl.BlockSpec((1, B_VDIM), lambda b, i: (0, 0))],
        out_specs=pl.BlockSpec((tq, B_WIDTH), lambda b, i: (b * nq + i, 0)),
        out_shape=jax.ShapeDtypeStruct((m, B_WIDTH), F32),
        compiler_params=_params("arbitrary", "arbitrary"),
        name="diff_prompt",
    )(qrot, kvb, kvb, lam_qk, g_sub)


def _sigmoid(x):
    return 1.0 / (1.0 + jnp.exp(-x))


def _merge_kernel(x_ref, oa_ref, ob_ref, za_ref, zb_ref, ga_ref, gb_ref, wa_ref, wb_ref, wo_ref, g_ref,
                  x_out_ref, n_out_ref):
    za, zb = za_ref[...], zb_ref[...]
    ya = (oa_ref[...] * (za * _sigmoid(za))).astype(BF16)
    yb = (ob_ref[...] * (zb * _sigmoid(zb))).astype(BF16)
    branch_a = jnp.dot(ya, wa_ref[...], preferred_element_type=F32)
    branch_b = jnp.dot(yb, wb_ref[...], preferred_element_type=F32)
    mix = _sigmoid(ga_ref[...]) * branch_a + _sigmoid(gb_ref[...]) * branch_b
    x_new = x_ref[...] + jnp.dot(mix.astype(BF16), wo_ref[...], preferred_element_type=F32)
    x_out_ref[...] = x_new
    ms = jnp.mean(x_new * x_new, axis=-1, keepdims=True)
    n_out_ref[...] = (x_new * lax.rsqrt(ms + EPS) * g_ref[...]).astype(n_out_ref.dtype)


def _merge(x, o_a, o_b, zg, w_pa, w_pb, w_o, g_next, norm_dtype):
    m, d = x.shape
    tm = min(MERGE_TILE, m)
    row = lambda width, col: pl.BlockSpec((tm, width), lambda i: (i, col))
    full = lambda a: pl.BlockSpec(a.shape, lambda i: (0, 0), pipeline_mode=pl.Buffered(1))
    return pl.pallas_call(
        _merge_kernel,
        grid=(m // tm,),
        in_specs=[row(d, 0), row(A_WIDTH, 0), row(B_WIDTH, 0),
                  row(A_WIDTH, 0), row(B_WIDTH, 1), row(d, 1), row(d, 2),
                  full(w_pa), full(w_pb), full(w_o), pl.BlockSpec((1, d), lambda i: (0, 0))],
        out_specs=[row(d, 0), row(d, 0)],
        out_shape=[jax.ShapeDtypeStruct((m, d), F32), jax.ShapeDtypeStruct((m, d), norm_dtype)],
        compiler_params=_params("arbitrary"),
        name="merge",
    )(x, o_a, o_b, zg, zg, zg, zg, w_pa, w_pb, w_o, g_next.reshape(1, d).astype(F32))


def _decode_kernel(pt_ref, qi_ref, w_ref, qa_ref, qb_ref, kin_ref, kan_ref, van_ref, kbn_ref, vbn_ref,
                   lq_ref, g_ref, cki_ref, cka_ref, cva_ref, ckb_ref, cvb_ref,
                   oa_ref, ob_ref, bki, bka, bva, bkb, bvb, sems, *, layer, past_len, n_new, k_top, lam_init):
    b = pl.program_id(0)
    nb = pl.num_programs(0)
    n_pages = bki.shape[1] - 1
    page = bki.shape[2]
    s_all = (n_pages + 1) * page
    pairs = ((cki_ref, bki), (cka_ref, bka), (cva_ref, bva), (ckb_ref, bkb), (cvb_ref, bvb))

    def page_copy(src, dst, which, seq, slot, p):
        return pltpu.make_async_copy(src.at[layer, pt_ref[seq, p]], dst.at[slot, p], sems.at[slot, which])

    def start_all(seq, slot):
        for p in range(n_pages):
            for which, (src, dst) in enumerate(pairs):
                page_copy(src, dst, which, seq, slot, p).start()

    @pl.when(b == 0)
    def _():
        start_all(0, 0)

    slot = lax.rem(b, 2)

    @pl.when(b + 1 < nb)
    def _():
        start_all(b + 1, 1 - slot)

    for buf, new in ((bki, kin_ref), (bka, kan_ref), (bva, van_ref), (bkb, kbn_ref), (bvb, vbn_ref)):
        buf[slot, n_pages] = jnp.zeros(buf.shape[2:], F32)
        buf[slot, n_pages, 0:DEC_ROWS, :] = new[0]

    for p in range(n_pages):
        for which, (src, dst) in enumerate(pairs):
            page_copy(src, dst, which, b, slot, p).wait()

    rows = DEC_ROWS
    col = lax.broadcasted_iota(I32, (rows, s_all), 1)
    tok = jnp.minimum(lax.broadcasted_iota(I32, (rows, s_all), 0), n_new - 1)
    causal = (col <= past_len + tok) & (col < past_len + n_new)

    kidx = bki[slot].reshape(s_all, IDX_DIM).astype(BF16)
    d = lax.dot_general(qi_ref[0], kidx, _NT, preferred_element_type=F32)
    w_idx = w_ref[0] * (IDX_DIM ** -0.5)
    acc = jnp.zeros((rows, s_all), F32)
    for h in range(IDX_HEADS):
        acc = acc + jnp.maximum(d[h * rows:(h + 1) * rows], 0.0) * w_idx[:, h:h + 1]
    key = jnp.where(causal, _float_key(acc), jnp.int32(INT_MIN))

    def count_ge(cand):
        return jnp.sum(jnp.where(key >= cand, 1.0, 0.0), axis=1, keepdims=True)

    thr = _search_threshold(count_ge, rows, float(k_top))
    c_ge = count_ge(thr)
    c_gt = jnp.sum(jnp.where(key > thr, 1.0, 0.0), axis=1, keepdims=True)
    need = float(k_top) - c_gt
    eq = key == thr
    tri = _strict_upper(LANES)
    run = jnp.zeros((rows, 1), F32)
    drops = []
    for j in range(s_all // LANES):
        eqj = jnp.where(eq[:, j * LANES:(j + 1) * LANES], 1.0, 0.0)
        before = jnp.dot(eqj.astype(BF16), tri, preferred_element_type=F32) + run
        drops.append(before >= need)
        run = run + jnp.sum(eqj, axis=1, keepdims=True)
    drop = eq & jnp.concatenate(drops, axis=1) & (c_ge > float(k_top))
    sel = ((key >= thr) & (key > jnp.int32(KEY_NEG_INF)) & (key < jnp.int32(KEY_POS_INF))
           & jnp.logical_not(drop))
    sel_bias = jnp.where(sel, 0.0, NEG_BIG)
    causal_bias = jnp.where(causal, 0.0, NEG_BIG)
    scale = HEAD_DIM ** -0.5

    def softmax_pv(s, v):
        m = jnp.max(s, axis=1, keepdims=True)
        p = jnp.exp(s - m)
        l = jnp.sum(p, axis=1, keepdims=True)
        return jnp.dot(p.astype(BF16), v, preferred_element_type=F32) / l

    ka = bka[slot].reshape(s_all, A_KV * HEAD_DIM)
    va = bva[slot].reshape(s_all, A_KV * HEAD_DIM)
    bias_a = jnp.concatenate([sel_bias] * A_GROUP, axis=0)
    for g in range(A_KV):
        kg = ka[:, g * HEAD_DIM:(g + 1) * HEAD_DIM].astype(BF16)
        vg = va[:, g * HEAD_DIM:(g + 1) * HEAD_DIM].astype(BF16)
        s = lax.dot_general(qa_ref[0, g], kg, _NT, preferred_element_type=F32) * scale + bias_a
        o = softmax_pv(s, vg)
        for r in range(A_GROUP):
            hh = g * A_GROUP + r
            oa_ref[0, :, hh * HEAD_DIM:(hh + 1) * HEAD_DIM] = o[r * rows:(r + 1) * rows]

    lam = _lambda_full(lq_ref, lam_init)
    kb = bkb[slot].reshape(s_all, B_KV * 2 * HEAD_DIM)
    vb = bvb[slot].reshape(s_all, B_KV * B_VDIM)
    bias_b = jnp.concatenate([causal_bias] * B_GROUP, axis=0)
    for g in range(B_KV):
        vg = vb[:, g * B_VDIM:(g + 1) * B_VDIM].astype(BF16)
        outs = []
        for c2 in range(2):
            kgc = kb[:, (g * 2 + c2) * HEAD_DIM:(g * 2 + c2 + 1) * HEAD_DIM].astype(BF16)
            s = lax.dot_general(qb_ref[0, g, c2], kgc, _NT, preferred_element_type=F32) * scale + bias_b
            outs.append(softmax_pv(s, vg))
        o = _subln(outs[0] - lam * outs[1], g_ref[...], lam_init)
        for r in range(B_GROUP):
            hh = g * B_GROUP + r
            ob_ref[0, :, hh * B_VDIM:(hh + 1) * B_VDIM] = o[r * rows:(r + 1) * rows]


def _decode_attend(page_table, qi8, w8, qa8, qb8, new_rows, lam_qk, g_sub, caches, layer, past_len, n_new,
                   k_top, lam_init):
    nseq, n_pages = page_table.shape
    page = caches[0].shape[2]
    seq_spec = lambda a: pl.BlockSpec((1,) + a.shape[1:], lambda b, pt: (b,) + (0,) * (a.ndim - 1))
    const = lambda a: pl.BlockSpec(a.shape, lambda b, pt: (0,) * a.ndim)
    any_spec = pl.BlockSpec(memory_space=pl.ANY)
    per_seq = [qi8, w8, qa8, qb8] + list(new_rows)
    buf = lambda c: pltpu.VMEM((2, n_pages + 1, page, c.shape[3]), F32)
    grid_spec = pltpu.PrefetchScalarGridSpec(
        num_scalar_prefetch=1,
        grid=(nseq,),
        in_specs=[seq_spec(a) for a in per_seq] + [const(lam_qk), const(g_sub)] + [any_spec] * 5,
        out_specs=[pl.BlockSpec((1, DEC_ROWS, A_WIDTH), lambda b, pt: (b, 0, 0)),
                   pl.BlockSpec((1, DEC_ROWS, B_WIDTH), lambda b, pt: (b, 0, 0))],
        scratch_shapes=[buf(c) for c in caches] + [pltpu.SemaphoreType.DMA((2, 5))],
    )
    return pl.pallas_call(
        functools.partial(_decode_kernel, layer=layer, past_len=past_len, n_new=n_new, k_top=k_top,
                          lam_init=lam_init),
        grid_spec=grid_spec,
        out_shape=[jax.ShapeDtypeStruct((nseq, DEC_ROWS, A_WIDTH), F32),
                   jax.ShapeDtypeStruct((nseq, DEC_ROWS, B_WIDTH), F32)],
        compiler_params=_params("arbitrary"),
        name="decode_attend",
    )(page_table, *per_seq, lam_qk, g_sub, *caches)


def _rope_tables(pos):
    def cs(dim):
        half = dim // 2
        inv = ROPE_THETA ** (-jnp.arange(half, dtype=F32) / half)
        ang = pos.astype(F32)[:, None] * inv[None, :]
        return jnp.cos(ang), jnp.sin(ang)
    cos, sin = cs(HEAD_DIM)
    c128 = jnp.concatenate([cos, cos], axis=1)
    s128 = jnp.concatenate([-sin, sin], axis=1)
    cos, sin = cs(IDX_DIM)
    zero = jnp.zeros_like(sin)
    c64 = jnp.concatenate([cos, cos, cos, cos], axis=1)
    lo64 = jnp.concatenate([-sin, zero, -sin, zero], axis=1)
    hi64 = jnp.concatenate([zero, sin, zero, sin], axis=1)
    n = pos.shape[0]
    tail = jnp.concatenate([jnp.full((n, IDX_HEADS), IDX_HEADS ** -0.5, F32),
                            jnp.zeros((n, LANES - IDX_DIM - IDX_HEADS), F32)], axis=1)
    ck = jnp.concatenate([cos, cos, tail], axis=1)
    zero_tail = jnp.zeros((n, LANES - IDX_DIM), F32)
    klo = jnp.concatenate([-sin, zero, zero_tail], axis=1)
    khi = jnp.concatenate([zero, sin, zero_tail], axis=1)
    return (c128, s128), (c64, lo64, hi64, ck, klo, khi)


def _split_weights(w):
    sizes = (A_WIDTH, A_KV * HEAD_DIM, A_KV * HEAD_DIM, A_WIDTH, IDX_HEADS * IDX_DIM, IDX_DIM, IDX_HEADS,
             B_HEADS * 2 * HEAD_DIM, B_KV * 2 * HEAD_DIM, B_KV * B_VDIM, B_WIDTH, w.shape[0], w.shape[0])
    pts = np.cumsum(sizes)[:-1]
    q_a, k_a, v_a, z_a, qi, ki, wi, q_b, k_b, v_b, z_b, g_a, g_b = jnp.split(w, pts, axis=1)
    pad = jnp.zeros((w.shape[0], LANES - IDX_DIM - IDX_HEADS), w.dtype)
    cat = lambda parts: jnp.concatenate(parts, axis=1).astype(BF16)
    return (cat([q_a, q_b]), cat([k_b, v_b, k_a, v_a]), cat([qi, ki, wi, pad]), cat([z_a, z_b, g_a, g_b]))


def _project_all(h, weights, tabs, nb, lp, n_keep, expand):
    w_q, w_kv, w_idx, w_zg = weights
    qrot = _proj_q(h, w_q, tabs[0])
    kv = _proj_kv(h.reshape(nb, lp, h.shape[1]), w_kv, tabs[0], n_keep)
    idx = _proj_idx(h, w_idx, tabs[1], expand)
    zg = _proj_plain(h, w_zg, 1024)
    return qrot, kv, idx, zg


def kernel(x_prompt, x_sample, cache_k_a, cache_v_a, cache_k_idx, cache_k_b, cache_v_b, page_table,
           meta_tokens, norm_g, w_in, lam_qk, subln_g, w_proj_a, w_proj_b, w_out, final_norm_g):
    nb, seq, d = x_prompt.shape
    nseq, n_new, _ = x_sample.shape
    depth = norm_g.shape[0]
    n_meta = meta_tokens.shape[0]
    n_pages = page_table.shape[1]
    page = cache_k_a.shape[2]
    past_len = n_pages * page
    n_tok = n_meta + seq
    lp = -(-n_tok // ROW_TILE) * ROW_TILE
    k_top_p = min(TOP_K_MAX, seq // 4)
    k_top_s = min(TOP_K_MAX, (past_len + n_new) // 4)

    meta = jnp.broadcast_to(meta_tokens.astype(x_prompt.dtype)[None], (nb, n_meta, d))
    xp = jnp.concatenate([meta, x_prompt, jnp.zeros((nb, lp - n_tok, d), x_prompt.dtype)], axis=1)
    xp = xp.reshape(nb * lp, d)
    xs = x_sample.reshape(nseq * n_new, d)
    tabs_p = _rope_tables(jnp.arange(lp, dtype=I32))
    tabs_s = _rope_tables(past_len + jnp.arange(nseq * n_new, dtype=I32) % n_new)

    np_, pg = cache_k_a.shape[1], cache_k_a.shape[2]
    caches = (cache_k_idx,
              cache_k_a.reshape(depth, np_, pg, A_KV * HEAD_DIM),
              cache_v_a.reshape(depth, np_, pg, A_KV * HEAD_DIM),
              cache_k_b.reshape(depth, np_, pg, B_KV * 2 * HEAD_DIM),
              cache_v_b.reshape(depth, np_, pg, B_KV * B_VDIM))

    hp = _rmsnorm(xp, norm_g[0], BF16)
    hs = _rmsnorm(xs, norm_g[0], BF16)
    rows_p = [[] for _ in range(5)]
    rows_s = [[] for _ in range(5)]
    pad_t = lambda a, axis: jnp.pad(a, [(0, DEC_ROWS - n_new) if ax == axis else (0, 0) for ax in range(a.ndim)])
    for l in range(depth):
        lam_init = _lambda_init(l)
        weights = _split_weights(w_in[l])
        w_pa, w_pb, w_o = w_proj_a[l].astype(BF16), w_proj_b[l].astype(BF16), w_out[l].astype(BF16)
        last = l + 1 == depth
        g_next = final_norm_g if last else norm_g[l + 1]
        norm_dtype = F32 if last else BF16
        lq = lam_qk[l].astype(F32)
        g_sub = subln_g[l].reshape(1, B_VDIM).astype(F32)

        qrot, (kvb, nkb, nvb, nka, nva), (qe, kw, kk), zg = _project_all(hp, weights, tabs_p, nb, lp, n_tok, True)
        o_a = _dsa_prompt(qrot, qe, kw, kk, kvb, nb, lp, k_top_p)
        o_b = _diff_prompt(qrot, kvb, lq, g_sub, nb, lp, lam_init)
        xp, hp = _merge(xp, o_a, o_b, zg, w_pa, w_pb, w_o, g_next, norm_dtype)
        ki_p = kw.reshape(nb, lp, LANES)[:, :n_tok, :IDX_DIM]
        for i, a in enumerate((nka.reshape(nb, n_tok, A_KV, HEAD_DIM), nva.reshape(nb, n_tok, A_KV, HEAD_DIM), ki_p,
                               nkb.reshape(nb, n_tok, B_KV, 2, HEAD_DIM), nvb.reshape(nb, n_tok, B_KV, B_VDIM))):
            rows_p[i].append(a)

        ms = nseq * n_new
        qrot, (_, nkb, nvb, nka, nva), (qi, kw, _), zg = _project_all(hs, weights, tabs_s, 1, ms, ms, False)
        qa8 = pad_t(qrot[:, :A_WIDTH].reshape(nseq, n_new, A_KV, A_GROUP, HEAD_DIM).transpose(0, 2, 3, 1, 4), 3)
        qa8 = qa8.reshape(nseq, A_KV, A_GROUP * DEC_ROWS, HEAD_DIM)
        qb8 = qrot[:, A_WIDTH:].reshape(nseq, n_new, B_KV, B_GROUP, 2, HEAD_DIM).transpose(0, 2, 4, 3, 1, 5)
        qb8 = pad_t(qb8, 4).reshape(nseq, B_KV, 2, B_GROUP * DEC_ROWS, HEAD_DIM)
        qi8 = pad_t(qi.reshape(nseq, n_new, IDX_HEADS, IDX_DIM).transpose(0, 2, 1, 3), 2)
        qi8 = qi8.reshape(nseq, IDX_HEADS * DEC_ROWS, IDX_DIM)
        w8 = pad_t(kw[:, IDX_DIM:IDX_DIM + IDX_HEADS].reshape(nseq, n_new, IDX_HEADS), 1)
        ki_s = kw[:, :IDX_DIM].reshape(nseq, n_new, IDX_DIM)
        new_rows = [pad_t(a.reshape(nseq, n_new, -1), 1) for a in (ki_s, nka, nva, nkb, nvb)]
        o_a8, o_b8 = _decode_attend(page_table, qi8, w8, qa8, qb8, new_rows, lq, g_sub, caches, l, past_len,
                                    n_new, k_top_s, lam_init)
        o_a = o_a8[:, :n_new].reshape(ms, A_WIDTH)
        o_b = o_b8[:, :n_new].reshape(ms, B_WIDTH)
        xs, hs = _merge(xs, o_a, o_b, zg, w_pa, w_pb, w_o, g_next, norm_dtype)
        for i, a in enumerate((nka.reshape(nseq, n_new, A_KV, HEAD_DIM), nva.reshape(nseq, n_new, A_KV, HEAD_DIM),
                               ki_s, nkb.reshape(nseq, n_new, B_KV, 2, HEAD_DIM),
                               nvb.reshape(nseq, n_new, B_KV, B_VDIM))):
            rows_s[i].append(a)

    y_prompt = hp.reshape(nb, lp, d)[:, n_meta:n_tok]
    y_sample = hs.reshape(nseq, n_new, d)
    return (y_prompt, y_sample) + tuple(jnp.stack(r) for r in rows_p) + tuple(jnp.stack(r) for r in rows_s)
```

```python
import functools
import math

import jax
import jax.numpy as jnp
import numpy as np
from jax import lax
from jax.experimental import pallas as pl
from jax.experimental.pallas import tpu as pltpu

HEAD_DIM = 128
A_HEADS = 8
A_KV = 2
A_GROUP = A_HEADS // A_KV
A_WIDTH = A_HEADS * HEAD_DIM
IDX_HEADS = 16
IDX_DIM = 64
B_HEADS = 4
B_KV = 2
B_GROUP = B_HEADS // B_KV
B_VDIM = 2 * HEAD_DIM
B_WIDTH = B_HEADS * B_VDIM
TOP_K_MAX = 256
ROPE_THETA = 10000.0
EPS = 1e-6

LANES = 128
ROW_TILE = 256
KEY_CHUNK = 256
DSA_Q_TILE = 256
SEARCH_ROWS = 128
DIFF_Q_TILE = 256
MERGE_TILE = 256
DEC_ROWS = 8
VMEM_LIMIT = 48 * 1024 * 1024

NEG_BIG = -1e30
INT_MIN = -2 ** 31
KEY_NEG_INF = -2139095041
KEY_POS_INF = 2139095040

F32 = jnp.float32
BF16 = jnp.bfloat16
I32 = jnp.int32

_NT = (((1,), (1,)), ((), ()))


def _params(*sem):
    return pltpu.CompilerParams(dimension_semantics=sem, vmem_limit_bytes=VMEM_LIMIT)


def _lambda_init(layer):
    return 0.8 - 0.6 * math.exp(-0.3 * layer)


def _float_key(x):
    b = lax.bitcast_convert_type(x, I32)
    return b ^ ((b >> 31) & jnp.int32(0x7FFFFFFF))


def _rmsnorm_kernel(x_ref, g_ref, o_ref):
    x = x_ref[...]
    ms = jnp.mean(x * x, axis=-1, keepdims=True)
    o_ref[...] = (x * lax.rsqrt(ms + EPS) * g_ref[...]).astype(o_ref.dtype)


def _rmsnorm(x, g, out_dtype):
    m, d = x.shape
    tm = min(ROW_TILE, m)
    return pl.pallas_call(
        _rmsnorm_kernel,
        grid=(m // tm,),
        in_specs=[pl.BlockSpec((tm, d), lambda i: (i, 0)),
                  pl.BlockSpec((1, d), lambda i: (0, 0))],
        out_specs=pl.BlockSpec((tm, d), lambda i: (i, 0)),
        out_shape=jax.ShapeDtypeStruct((m, d), out_dtype),
        compiler_params=_params("arbitrary"),
        name="rmsnorm",
    )(x, g.reshape(1, d).astype(F32))


def _rope128(a, c, s):
    return a * c + pltpu.roll(a, HEAD_DIM // 2, 1) * s


def _rope64(a, c, s_lo, s_hi):
    return a * c + pltpu.roll(a, LANES - IDX_DIM // 2, 1) * s_lo + pltpu.roll(a, IDX_DIM // 2, 1) * s_hi


def _proj_q_kernel(h_ref, w_ref, c_ref, s_ref, o_ref):
    acc = jnp.dot(h_ref[...], w_ref[...], preferred_element_type=F32)
    c, s = c_ref[...], s_ref[...]
    for j in range(acc.shape[1] // LANES):
        sl = slice(j * LANES, (j + 1) * LANES)
        o_ref[:, sl] = _rope128(acc[:, sl], c, s).astype(o_ref.dtype)


def _proj_kv_kernel(h_ref, w_ref, c_ref, s_ref, kv_ref, kb_ref, vb_ref, ka_ref, va_ref):
    acc = jnp.dot(h_ref[0], w_ref[...], preferred_element_type=F32)
    c, s = c_ref[...], s_ref[...]
    outs = ((kb_ref, 0, 4, True), (vb_ref, 4, 4, False), (ka_ref, 8, 2, True), (va_ref, 10, 2, False))
    for ref, start, count, rot in outs:
        for j in range(count):
            a = acc[:, (start + j) * LANES:(start + j + 1) * LANES]
            if rot:
                a = _rope128(a, c, s)
            ref[0, :, j * LANES:(j + 1) * LANES] = a
            kv_ref[0, :, (start + j) * LANES:(start + j + 1) * LANES] = a.astype(kv_ref.dtype)


def _proj_idx_kernel(h_ref, w_ref, c_ref, slo_ref, shi_ref, ck_ref, klo_ref, khi_ref,
                     q_ref, kw_ref, kk_ref, *, expand):
    acc = jnp.dot(h_ref[...], w_ref[...], preferred_element_type=F32)
    c, slo, shi = c_ref[...], slo_ref[...], shi_ref[...]
    lane = lax.broadcasted_iota(I32, (acc.shape[0], LANES), 1)
    low = lane < IDX_DIM
    for j in range(IDX_HEADS // 2):
        a = _rope64(acc[:, j * LANES:(j + 1) * LANES], c, slo, shi)
        if expand:
            q_ref[:, (2 * j) * LANES:(2 * j + 1) * LANES] = jnp.where(low, a, 0.0).astype(q_ref.dtype)
            q_ref[:, (2 * j + 1) * LANES:(2 * j + 2) * LANES] = jnp.where(low, 0.0, a).astype(q_ref.dtype)
        else:
            q_ref[:, j * LANES:(j + 1) * LANES] = a.astype(q_ref.dtype)
    kw = _rope64(acc[:, IDX_HEADS * IDX_DIM:], ck_ref[...], klo_ref[...], khi_ref[...])
    kw_ref[...] = kw
    kk_ref[...] = jnp.where(low, kw, pltpu.roll(kw, IDX_DIM, 1)).astype(kk_ref.dtype)


def _proj_plain_kernel(h_ref, w_ref, o_ref):
    o_ref[...] = jnp.dot(h_ref[...], w_ref[...], preferred_element_type=F32).astype(o_ref.dtype)


def _row_spec(tm, width, ntab=None):
    if ntab is None:
        return pl.BlockSpec((tm, width), lambda j, i: (i, 0))
    return pl.BlockSpec((tm, width), lambda j, i: (i % ntab, 0))


def _proj_q(h, w, tabs):
    m, d = h.shape
    n = w.shape[1]
    tm = min(ROW_TILE, m)
    ntab = tabs[0].shape[0] // tm
    return pl.pallas_call(
        _proj_q_kernel,
        grid=(1, m // tm),
        in_specs=[_row_spec(tm, d), pl.BlockSpec((d, n), lambda j, i: (0, 0)),
                  _row_spec(tm, LANES, ntab), _row_spec(tm, LANES, ntab)],
        out_specs=_row_spec(tm, n),
        out_shape=jax.ShapeDtypeStruct((m, n), BF16),
        compiler_params=_params("arbitrary", "arbitrary"),
        name="proj_q",
    )(h, w, tabs[0], tabs[1])


def _proj_kv(h3, w, tabs, n_keep):
    nb, lp, d = h3.shape
    n = w.shape[1]
    tm = min(DIFF_Q_TILE, lp)
    row = lambda width: pl.BlockSpec((1, tm, width), lambda b, i: (b, i, 0))
    tab = pl.BlockSpec((tm, LANES), lambda b, i: (i, 0))
    f32_out = lambda width: jax.ShapeDtypeStruct((nb, n_keep, width), F32)
    return pl.pallas_call(
        _proj_kv_kernel,
        grid=(nb, lp // tm),
        in_specs=[row(d), pl.BlockSpec((d, n), lambda b, i: (0, 0)), tab, tab],
        out_specs=[row(n), row(4 * LANES), row(4 * LANES), row(2 * LANES), row(2 * LANES)],
        out_shape=[jax.ShapeDtypeStruct((nb, lp, n), BF16),
                   f32_out(4 * LANES), f32_out(4 * LANES), f32_out(2 * LANES), f32_out(2 * LANES)],
        compiler_params=_params("arbitrary", "arbitrary"),
        name="proj_kv",
    )(h3, w, tabs[0], tabs[1])


def _proj_idx(h, w, tabs, expand):
    m, d = h.shape
    n = w.shape[1]
    tm = min(ROW_TILE, m)
    ntab = tabs[0].shape[0] // tm
    qw = IDX_HEADS * (LANES if expand else IDX_DIM)
    return pl.pallas_call(
        functools.partial(_proj_idx_kernel, expand=expand),
        grid=(1, m // tm),
        in_specs=[_row_spec(tm, d), pl.BlockSpec((d, n), lambda j, i: (0, 0))]
                 + [_row_spec(tm, LANES, ntab)] * 6,
        out_specs=[_row_spec(tm, qw), _row_spec(tm, LANES), _row_spec(tm, LANES)],
        out_shape=[jax.ShapeDtypeStruct((m, qw), BF16),
                   jax.ShapeDtypeStruct((m, LANES), F32),
                   jax.ShapeDtypeStruct((m, LANES), BF16)],
        compiler_params=_params("arbitrary", "arbitrary"),
        name="proj_idx",
    )(h, w, *tabs)


def _proj_plain(h, w, tn):
    m, d = h.shape
    n = w.shape[1]
    tm = min(ROW_TILE, m)
    return pl.pallas_call(
        _proj_plain_kernel,
        grid=(n // tn, m // tm),
        in_specs=[_row_spec(tm, d), pl.BlockSpec((d, tn), lambda j, i: (0, j))],
        out_specs=pl.BlockSpec((tm, tn), lambda j, i: (i, j)),
        out_shape=jax.ShapeDtypeStruct((m, n), F32),
        compiler_params=_params("arbitrary", "arbitrary"),
        name="proj_gate",
    )(h, w)


def _search_threshold(count_ge, rows, k_top):
    zero = jnp.zeros((rows, 1), I32)
    t = jnp.where(count_ge(zero) >= k_top, zero, jnp.int32(INT_MIN))

    def bit_body(bi, t):
        cand = t | lax.shift_left(jnp.int32(1), 30 - bi)
        return jnp.where(count_ge(cand) >= k_top, cand, t)
    return lax.fori_loop(0, 31, bit_body, t)


def _strict_upper(n):
    r = lax.broadcasted_iota(I32, (n, n), 0)
    c = lax.broadcasted_iota(I32, (n, n), 1)
    return jnp.where(r < c, 1.0, 0.0).astype(BF16)


def _dsa_kernel(qa_ref, qe_ref, kw_ref, kk_ref, ka_ref, va_ref, o_ref, sk_ref, *, k_top):
    tq = qa_ref.shape[0]
    kc = KEY_CHUNK
    hr = SEARCH_ROWS
    i = pl.program_id(1)
    q0 = i * tq
    nkc = (q0 + tq + kc - 1) // kc
    col_iota = lax.broadcasted_iota(I32, (hr, kc), 1)
    row_iota = lax.broadcasted_iota(I32, (hr, kc), 0)

    thr_slabs = []
    for r0 in range(0, tq, hr):
        w_idx = kw_ref[r0:r0 + hr, IDX_DIM:IDX_DIM + IDX_HEADS] * (IDX_DIM ** -0.5)
        w_cols = [jnp.broadcast_to(w_idx[:, h:h + 1], (hr, LANES)) for h in range(IDX_HEADS)]
        q_stack = jnp.concatenate([qe_ref[r0:r0 + hr, h * LANES:(h + 1) * LANES] for h in range(IDX_HEADS)],
                                  axis=0)

        def score_chunk(c, q_stack=q_stack, w_cols=w_cols):
            k0 = pl.multiple_of(c * kc, kc)
            d = lax.dot_general(q_stack, kk_ref[pl.ds(k0, kc), :], _NT, preferred_element_type=F32)
            parts = []
            for j in range(kc // LANES):
                acc = jnp.zeros((hr, LANES), F32)
                for h in range(IDX_HEADS):
                    acc = acc + jnp.maximum(d[h * hr:(h + 1) * hr, j * LANES:(j + 1) * LANES], 0.0) * w_cols[h]
                parts.append(acc)
            return _float_key(jnp.concatenate(parts, axis=1)), k0

        def p1_body(c, carry, r0=r0, score_chunk=score_chunk):
            key, _ = score_chunk(c)
            sk_ref[c, r0:r0 + hr, :] = key
            return carry
        lax.fori_loop(0, nkc - 1, p1_body, 0)
        key, k0 = score_chunk(nkc - 1)
        sk_ref[nkc - 1, r0:r0 + hr, :] = jnp.where(k0 + col_iota <= q0 + r0 + row_iota, key, jnp.int32(INT_MIN))

        def count_cmp(cand, strict, r0=r0):
            cand_b = jnp.broadcast_to(cand, (hr, LANES))
            def body(c, acc):
                for j in range(kc // LANES):
                    kj = sk_ref[c, r0:r0 + hr, j * LANES:(j + 1) * LANES]
                    hit = (kj > cand_b) if strict else (kj >= cand_b)
                    acc = acc + jnp.where(hit, 1.0, 0.0)
                return acc
            acc = lax.fori_loop(0, nkc, body, jnp.zeros((hr, LANES), F32))
            return jnp.sum(acc, axis=1, keepdims=True)

        thr = _search_threshold(lambda cand: count_cmp(cand, False), hr, float(k_top))
        c_ge = count_cmp(thr, False)
        c_gt = count_cmp(thr, True)
        need = float(k_top) - c_gt
        tie_rows = jnp.where((c_ge > float(k_top)) & (thr > jnp.int32(INT_MIN)), 1.0, 0.0)

        @pl.when(jnp.max(tie_rows) > 0.0)
        def _(r0=r0, thr=thr, need=need):
            tri = _strict_upper(kc)
            def body(c, run):
                k = sk_ref[c, r0:r0 + hr, :]
                eq = k == thr
                eqf = jnp.where(eq, 1.0, 0.0)
                before = jnp.dot(eqf.astype(BF16), tri, preferred_element_type=F32) + run
                sk_ref[c, r0:r0 + hr, :] = jnp.where(eq & (before >= need), jnp.int32(INT_MIN), k)
                return run + jnp.sum(eqf, axis=1, keepdims=True)
            lax.fori_loop(0, nkc, body, jnp.zeros((hr, 1), F32))

        thr_slabs.append(thr)
    thr = jnp.concatenate(thr_slabs, axis=0)

    scale = HEAD_DIM ** -0.5

    def body(c, carry):
        k0 = pl.multiple_of(c * kc, kc)
        key = sk_ref[c]
        sel = (key >= thr) & (key > jnp.int32(KEY_NEG_INF)) & (key < jnp.int32(KEY_POS_INF))
        bias = jnp.where(sel, 0.0, NEG_BIG)
        out = []
        for g in range(A_KV):
            qg = jnp.concatenate([qa_ref[:, (g * A_GROUP + r) * HEAD_DIM:(g * A_GROUP + r + 1) * HEAD_DIM]
                                  for r in range(A_GROUP)], axis=0)
            kch = ka_ref[0, pl.ds(k0, kc), g * HEAD_DIM:(g + 1) * HEAD_DIM]
            vch = va_ref[0, pl.ds(k0, kc), g * HEAD_DIM:(g + 1) * HEAD_DIM]
            s = lax.dot_general(qg, kch, _NT, preferred_element_type=F32)
            ps, stats = [], []
            for r in range(A_GROUP):
                m, l, _ = carry[g * A_GROUP + r]
                sr = s[r * tq:(r + 1) * tq] * scale + bias
                m_new = jnp.maximum(m, jnp.max(sr, axis=1, keepdims=True))
                p = jnp.exp(sr - m_new)
                alpha = jnp.exp(m - m_new)
                stats.append((m_new, alpha * l + jnp.sum(p, axis=1, keepdims=True), alpha))
                ps.append(p.astype(BF16))
            pv = jnp.dot(jnp.concatenate(ps, axis=0), vch, preferred_element_type=F32)
            for r in range(A_GROUP):
                m_new, l_new, alpha = stats[r]
                out.append((m_new, l_new, alpha * carry[g * A_GROUP + r][2] + pv[r * tq:(r + 1) * tq]))
        return tuple(out)

    init = tuple((jnp.full((tq, 1), NEG_BIG, F32), jnp.zeros((tq, 1), F32),
                  jnp.zeros((tq, HEAD_DIM), F32)) for _ in range(A_HEADS))
    fin = lax.fori_loop(0, nkc, body, init)
    for hh in range(A_HEADS):
        o_ref[:, hh * HEAD_DIM:(hh + 1) * HEAD_DIM] = fin[hh][2] / fin[hh][1]


def _dsa_prompt(qrot, qe, kw, kk, kvb, nb, lp, k_top):
    m = qrot.shape[0]
    tq = DSA_Q_TILE
    nq = lp // tq
    qrow = lambda width: pl.BlockSpec((tq, width), lambda b, i: (b * nq + i, 0))
    return pl.pallas_call(
        functools.partial(_dsa_kernel, k_top=k_top),
        grid=(nb, nq),
        in_specs=[qrow(A_WIDTH), qrow(IDX_HEADS * LANES), qrow(LANES),
                  pl.BlockSpec((lp, LANES), lambda b, i: (b, 0)),
                  pl.BlockSpec((1, lp, A_KV * HEAD_DIM), lambda b, i: (b, 0, 4)),
                  pl.BlockSpec((1, lp, A_KV * HEAD_DIM), lambda b, i: (b, 0, 5))],
        out_specs=qrow(A_WIDTH),
        out_shape=jax.ShapeDtypeStruct((m, A_WIDTH), F32),
        scratch_shapes=[pltpu.VMEM((lp // KEY_CHUNK, tq, KEY_CHUNK), I32)],
        compiler_params=_params("arbitrary", "arbitrary"),
        name="dsa_prompt",
    )(qrot, qe, kw, kk, kvb, kvb)


def _lambda_full(lq_ref, lam_init):
    lq = lq_ref[...]
    a = jnp.sum(lq[0:1] * lq[1:2], axis=1, keepdims=True)
    b = jnp.sum(lq[2:3] * lq[3:4], axis=1, keepdims=True)
    return jnp.exp(a) - jnp.exp(b) + lam_init


def _subln(o, g, lam_init):
    ms = jnp.mean(o * o, axis=-1, keepdims=True)
    return (o * lax.rsqrt(ms + EPS) * g) * (1.0 - lam_init)


def _diff_kernel(qb_ref, kb_ref, vb_ref, lq_ref, g_ref, o_ref, *, lam_init):
    tq = qb_ref.shape[0]
    kc = KEY_CHUNK
    i = pl.program_id(1)
    q0 = i * tq
    n_full = q0 // kc
    n_all = (q0 + tq + kc - 1) // kc
    lam = _lambda_full(lq_ref, lam_init)
    scale = HEAD_DIM ** -0.5
    rows = B_GROUP * tq
    row_pos = q0 + lax.broadcasted_iota(I32, (tq, kc), 0)
    col_iota = lax.broadcasted_iota(I32, (tq, kc), 1)
    row_pos = jnp.concatenate([row_pos] * B_GROUP, axis=0)
    col_iota = jnp.concatenate([col_iota] * B_GROUP, axis=0)

    combos = [(g, c2) for g in range(B_KV) for c2 in range(2)]

    def step(c, carry, masked):
        k0 = pl.multiple_of(c * kc, kc)
        out = []
        for idx, (g, c2) in enumerate(combos):
            m, l, acc = carry[idx]
            qgc = jnp.concatenate(
                [qb_ref[:, ((g * B_GROUP + r) * 2 + c2) * HEAD_DIM:((g * B_GROUP + r) * 2 + c2 + 1) * HEAD_DIM]
                 for r in range(B_GROUP)], axis=0)
            kcol = (g * 2 + c2) * HEAD_DIM
            kch = kb_ref[0, pl.ds(k0, kc), kcol:kcol + HEAD_DIM]
            vch = vb_ref[0, pl.ds(k0, kc), g * B_VDIM:(g + 1) * B_VDIM]
            s = lax.dot_general(qgc, kch, _NT, preferred_element_type=F32) * scale
            if masked:
                s = jnp.where(k0 + col_iota <= row_pos, s, NEG_BIG)
            m_new = jnp.maximum(m, jnp.max(s, axis=1, keepdims=True))
            p = jnp.exp(s - m_new)
            alpha = jnp.exp(m - m_new)
            l = alpha * l + jnp.sum(p, axis=1, keepdims=True)
            acc = alpha * acc + jnp.dot(p.astype(BF16), vch, preferred_element_type=F32)
            out.append((m_new, l, acc))
        return tuple(out)

    init = tuple((jnp.full((rows, 1), NEG_BIG, F32), jnp.zeros((rows, 1), F32),
                  jnp.zeros((rows, B_VDIM), F32)) for _ in combos)
    carry = lax.fori_loop(0, n_full, functools.partial(step, masked=False), init)
    carry = lax.fori_loop(n_full, n_all, functools.partial(step, masked=True), carry)
    for g in range(B_KV):
        o0, o1 = (carry[g * 2 + c2][2] / carry[g * 2 + c2][1] for c2 in range(2))
        o = _subln(o0 - lam * o1, g_ref[...], lam_init)
        for r in range(B_GROUP):
            hh = g * B_GROUP + r
            o_ref[:, hh * B_VDIM:(hh + 1) * B_VDIM] = o[r * tq:(r + 1) * tq]


def _diff_prompt(qrot, kvb, lam_qk, g_sub, nb, lp, lam_init):
    m = qrot.shape[0]
    tq = DIFF_Q_TILE
    nq = lp // tq
    return pl.pallas_call(
        functools.partial(_diff_kernel, lam_init=lam_init),
        grid=(nb, nq),
        in_specs=[pl.BlockSpec((tq, B_HEADS * 2 * HEAD_DIM), lambda b, i: (b * nq + i, 1)),
                  pl.BlockSpec((1, lp, B_KV * 2 * HEAD_DIM), lambda b, i: (b, 0, 0)),
                  pl.BlockSpec((1, lp, B_KV * B_VDIM), lambda b, i: (b, 0, 1)),
                  pl.BlockSpec((4, HEAD_DIM), lambda b, i: (0, 0)),
                  pl.BlockSpec((1, B_VDIM), lambda b, i: (0, 0))],
        out_specs=pl.BlockSpec((tq, B_WIDTH), lambda b, i: (b * nq + i, 0)),
        out_shape=jax.ShapeDtypeStruct((m, B_WIDTH), F32),
        compiler_params=_params("arbitrary", "arbitrary"),
        name="diff_prompt",
    )(qrot, kvb, kvb, lam_qk, g_sub)


def _sigmoid(x):
    return 1.0 / (1.0 + jnp.exp(-x))


def _merge_kernel(x_ref, oa_ref, ob_ref, za_ref, zb_ref, ga_ref, gb_ref, wa_ref, wb_ref, wo_ref, g_ref,
                  x_out_ref, n_out_ref):
    za, zb = za_ref[...], zb_ref[...]
    ya = (oa_ref[...] * (za * _sigmoid(za))).astype(BF16)
    yb = (ob_ref[...] * (zb * _sigmoid(zb))).astype(BF16)
    branch_a = jnp.dot(ya, wa_ref[...], preferred_element_type=F32)
    branch_b = jnp.dot(yb, wb_ref[...], preferred_element_type=F32)
    mix = _sigmoid(ga_ref[...]) * branch_a + _sigmoid(gb_ref[...]) * branch_b
    x_new = x_ref[...] + jnp.dot(mix.astype(BF16), wo_ref[...], preferred_element_type=F32)
    x_out_ref[...] = x_new
    ms = jnp.mean(x_new * x_new, axis=-1, keepdims=True)
    n_out_ref[...] = (x_new * lax.rsqrt(ms + EPS) * g_ref[...]).astype(n_out_ref.dtype)


def _merge(x, o_a, o_b, zg, w_pa, w_pb, w_o, g_next, norm_dtype):
    m, d = x.shape
    tm = min(MERGE_TILE, m)
    row = lambda width, col: pl.BlockSpec((tm, width), lambda i: (i, col))
    full = lambda a: pl.BlockSpec(a.shape, lambda i: (0, 0), pipeline_mode=pl.Buffered(1))
    return pl.pallas_call(
        _merge_kernel,
        grid=(m // tm,),
        in_specs=[row(d, 0), row(A_WIDTH, 0), row(B_WIDTH, 0),
                  row(A_WIDTH, 0), row(B_WIDTH, 1), row(d, 1), row(d, 2),
                  full(w_pa), full(w_pb), full(w_o), pl.BlockSpec((1, d), lambda i: (0, 0))],
        out_specs=[row(d, 0), row(d, 0)],
        out_shape=[jax.ShapeDtypeStruct((m, d), F32), jax.ShapeDtypeStruct((m, d), norm_dtype)],
        compiler_params=_params("arbitrary"),
        name="merge",
    )(x, o_a, o_b, zg, zg, zg, zg, w_pa, w_pb, w_o, g_next.reshape(1, d).astype(F32))


def _decode_kernel(pt_ref, qi_ref, w_ref, qa_ref, qb_ref, kin_ref, kan_ref, van_ref, kbn_ref, vbn_ref,
                   lq_ref, g_ref, cki_ref, cka_ref, cva_ref, ckb_ref, cvb_ref,
                   oa_ref, ob_ref, bki, bka, bva, bkb, bvb, sems, *, layer, past_len, n_new, k_top, lam_init):
    b = pl.program_id(0)
    nb = pl.num_programs(0)
    n_pages = bki.shape[1] - 1
    page = bki.shape[3]
    s_all = (n_pages + 1) * page
    pairs = ((cki_ref, bki), (cka_ref, bka), (cva_ref, bva), (ckb_ref, bkb), (cvb_ref, bvb))

    def page_copy(src, dst, which, seq, slot, p):
        return pltpu.make_async_copy(src.at[layer, pt_ref[seq, p]], dst.at[slot, p], sems.at[slot, which])

    def start_all(seq, slot):
        for p in range(n_pages):
            for which, (src, dst) in enumerate(pairs):
                page_copy(src, dst, which, seq, slot, p).start()

    @pl.when(b == 0)
    def _():
        start_all(0, 0)

    slot = lax.rem(b, 2)

    @pl.when(b + 1 < nb)
    def _():
        start_all(b + 1, 1 - slot)

    for buf, new in ((bki, kin_ref), (bka, kan_ref), (bva, van_ref), (bkb, kbn_ref), (bvb, vbn_ref)):
        buf[slot, n_pages] = jnp.zeros(buf.shape[2:], F32)
        buf[slot, n_pages, 0:new.shape[1]] = new[0]

    for p in range(n_pages):
        for which, (src, dst) in enumerate(pairs):
            page_copy(src, dst, which, b, slot, p).wait()

    rows = DEC_ROWS
    col = lax.broadcasted_iota(I32, (rows, s_all), 1)
    tok = jnp.minimum(lax.broadcasted_iota(I32, (rows, s_all), 0), n_new - 1)
    causal = (col <= past_len + tok) & (col < past_len + n_new)

    kidx_t = jnp.concatenate([bki[slot, p] for p in range(n_pages + 1)], axis=1).astype(BF16)
    d = jnp.dot(qi_ref[0], kidx_t, preferred_element_type=F32)
    w_idx = w_ref[0] * (IDX_DIM ** -0.5)
    acc = jnp.zeros((rows, s_all), F32)
    for h in range(IDX_HEADS):
        acc = acc + jnp.maximum(d[h * rows:(h + 1) * rows], 0.0) * w_idx[:, h:h + 1]
    key = jnp.where(causal, _float_key(acc), jnp.int32(INT_MIN))

    def count_ge(cand):
        return jnp.sum(jnp.where(key >= cand, 1.0, 0.0), axis=1, keepdims=True)

    thr = _search_threshold(count_ge, rows, float(k_top))
    c_ge = count_ge(thr)
    c_gt = jnp.sum(jnp.where(key > thr, 1.0, 0.0), axis=1, keepdims=True)
    need = float(k_top) - c_gt
    sel = (key >= thr) & (key > jnp.int32(KEY_NEG_INF)) & (key < jnp.int32(KEY_POS_INF))
    sel_bias = jnp.where(sel, 0.0, NEG_BIG)

    def drop_ties(sel_bias):
        eq = key == thr
        tri = _strict_upper(LANES)
        run = jnp.zeros((rows, 1), F32)
        drops = []
        for j in range(s_all // LANES):
            eqj = jnp.where(eq[:, j * LANES:(j + 1) * LANES], 1.0, 0.0)
            before = jnp.dot(eqj.astype(BF16), tri, preferred_element_type=F32) + run
            drops.append(before >= need)
            run = run + jnp.sum(eqj, axis=1, keepdims=True)
        drop = eq & jnp.concatenate(drops, axis=1) & (c_ge > float(k_top))
        return jnp.where(drop, NEG_BIG, sel_bias)

    tie_rows = jnp.where((c_ge > float(k_top)) & (thr > jnp.int32(INT_MIN)), 1.0, 0.0)
    sel_bias = lax.cond(jnp.max(tie_rows) > 0.0, drop_ties, lambda x: x, sel_bias)
    causal_bias = jnp.where(causal, 0.0, NEG_BIG)
    scale = HEAD_DIM ** -0.5

    def softmax_pv(s, v):
        m = jnp.max(s, axis=1, keepdims=True)
        p = jnp.exp(s - m)
        l = jnp.sum(p, axis=1, keepdims=True)
        return jnp.dot(p.astype(v.dtype), v, preferred_element_type=F32) / l

    bias_a = jnp.concatenate([sel_bias] * A_GROUP, axis=0)
    for g in range(A_KV):
        kg = bka[slot, :, pl.ds(g, page, stride=A_KV), :].reshape(s_all, HEAD_DIM).astype(BF16)
        vg = bva[slot, :, pl.ds(g, page, stride=A_KV), :].reshape(s_all, HEAD_DIM).astype(BF16)
        s = lax.dot_general(qa_ref[0, g], kg, _NT, preferred_element_type=F32) * scale + bias_a
        o = softmax_pv(s, vg)
        for r in range(A_GROUP):
            hh = g * A_GROUP + r
            oa_ref[0, :, hh * HEAD_DIM:(hh + 1) * HEAD_DIM] = o[r * rows:(r + 1) * rows]

    lam = _lambda_full(lq_ref, lam_init)
    bias_b = jnp.concatenate([causal_bias] * B_GROUP, axis=0)
    for g in range(B_KV):
        vg = jnp.concatenate(
            [bvb[slot, :, pl.ds(hf * B_KV + g, page, stride=2 * B_KV), :].reshape(s_all, HEAD_DIM)
             for hf in range(2)], axis=1).astype(BF16)
        outs = []
        for c2 in range(2):
            kgc = bkb[slot, :, pl.ds(g * 2 + c2, page, stride=2 * B_KV), :].reshape(s_all, HEAD_DIM).astype(BF16)
            s = lax.dot_general(qb_ref[0, g, c2], kgc, _NT, preferred_element_type=F32) * scale + bias_b
            outs.append(softmax_pv(s, vg))
        o = _subln(outs[0] - lam * outs[1], g_ref[...], lam_init)
        for r in range(B_GROUP):
            hh = g * B_GROUP + r
            ob_ref[0, :, hh * B_VDIM:(hh + 1) * B_VDIM] = o[r * rows:(r + 1) * rows]


def _decode_attend(page_table, qi8, w8, qa8, qb8, new_rows, lam_qk, g_sub, caches, layer, past_len, n_new,
                   k_top, lam_init):
    nseq, n_pages = page_table.shape
    page = caches[0].shape[2]
    seq_spec = lambda a: pl.BlockSpec((1,) + a.shape[1:], lambda b, pt: (b,) + (0,) * (a.ndim - 1))
    const = lambda a: pl.BlockSpec(a.shape, lambda b, pt: (0,) * a.ndim)
    any_spec = pl.BlockSpec(memory_space=pl.ANY)
    per_seq = [qi8, w8, qa8, qb8] + list(new_rows)
    buf = lambda c: pltpu.VMEM((2, n_pages + 1) + c.shape[2:], F32)
    grid_spec = pltpu.PrefetchScalarGridSpec(
        num_scalar_prefetch=1,
        grid=(nseq,),
        in_specs=[seq_spec(a) for a in per_seq] + [const(lam_qk), const(g_sub)] + [any_spec] * 5,
        out_specs=[pl.BlockSpec((1, DEC_ROWS, A_WIDTH), lambda b, pt: (b, 0, 0)),
                   pl.BlockSpec((1, DEC_ROWS, B_WIDTH), lambda b, pt: (b, 0, 0))],
        scratch_shapes=[buf(c) for c in caches] + [pltpu.SemaphoreType.DMA((2, 5))],
    )
    return pl.pallas_call(
        functools.partial(_decode_kernel, layer=layer, past_len=past_len, n_new=n_new, k_top=k_top,
                          lam_init=lam_init),
        grid_spec=grid_spec,
        out_shape=[jax.ShapeDtypeStruct((nseq, DEC_ROWS, A_WIDTH), F32),
                   jax.ShapeDtypeStruct((nseq, DEC_ROWS, B_WIDTH), F32)],
        compiler_params=_params("arbitrary"),
        name="decode_attend",
    )(page_table, *per_seq, lam_qk, g_sub, *caches)


def _rope_tables(pos):
    def cs(dim):
        half = dim // 2
        inv = ROPE_THETA ** (-jnp.arange(half, dtype=F32) / half)
        ang = pos.astype(F32)[:, None] * inv[None, :]
        return jnp.cos(ang), jnp.sin(ang)
    cos, sin = cs(HEAD_DIM)
    c128 = jnp.concatenate([cos, cos], axis=1)
    s128 = jnp.concatenate([-sin, sin], axis=1)
    cos, sin = cs(IDX_DIM)
    zero = jnp.zeros_like(sin)
    c64 = jnp.concatenate([cos, cos, cos, cos], axis=1)
    lo64 = jnp.concatenate([-sin, zero, -sin, zero], axis=1)
    hi64 = jnp.concatenate([zero, sin, zero, sin], axis=1)
    n = pos.shape[0]
    tail = jnp.concatenate([jnp.full((n, IDX_HEADS), IDX_HEADS ** -0.5, F32),
                            jnp.zeros((n, LANES - IDX_DIM - IDX_HEADS), F32)], axis=1)
    ck = jnp.concatenate([cos, cos, tail], axis=1)
    zero_tail = jnp.zeros((n, LANES - IDX_DIM), F32)
    klo = jnp.concatenate([-sin, zero, zero_tail], axis=1)
    khi = jnp.concatenate([zero, sin, zero_tail], axis=1)
    return (c128, s128), (c64, lo64, hi64, ck, klo, khi)


def _split_weights(w):
    sizes = (A_WIDTH, A_KV * HEAD_DIM, A_KV * HEAD_DIM, A_WIDTH, IDX_HEADS * IDX_DIM, IDX_DIM, IDX_HEADS,
             B_HEADS * 2 * HEAD_DIM, B_KV * 2 * HEAD_DIM, B_KV * B_VDIM, B_WIDTH, w.shape[0], w.shape[0])
    pts = np.cumsum(sizes)[:-1]
    q_a, k_a, v_a, z_a, qi, ki, wi, q_b, k_b, v_b, z_b, g_a, g_b = jnp.split(w, pts, axis=1)
    pad = jnp.zeros((w.shape[0], LANES - IDX_DIM - IDX_HEADS), w.dtype)
    cat = lambda parts: jnp.concatenate(parts, axis=1).astype(BF16)
    return (cat([q_a, q_b]), cat([k_b, v_b, k_a, v_a]), cat([qi, ki, wi, pad]), cat([z_a, z_b, g_a, g_b]))


def _project_all(h, weights, tabs, nb, lp, n_keep, expand):
    w_q, w_kv, w_idx, w_zg = weights
    qrot = _proj_q(h, w_q, tabs[0])
    kv = _proj_kv(h.reshape(nb, lp, h.shape[1]), w_kv, tabs[0], n_keep)
    idx = _proj_idx(h, w_idx, tabs[1], expand)
    zg = _proj_plain(h, w_zg, 1024)
    return qrot, kv, idx, zg


def kernel(x_prompt, x_sample, cache_k_a, cache_v_a, cache_k_idx, cache_k_b, cache_v_b, page_table,
           meta_tokens, norm_g, w_in, lam_qk, subln_g, w_proj_a, w_proj_b, w_out, final_norm_g):
    nb, seq, d = x_prompt.shape
    nseq, n_new, _ = x_sample.shape
    depth = norm_g.shape[0]
    n_meta = meta_tokens.shape[0]
    n_pages = page_table.shape[1]
    page = cache_k_a.shape[2]
    past_len = n_pages * page
    n_tok = n_meta + seq
    lp = -(-n_tok // ROW_TILE) * ROW_TILE
    k_top_p = min(TOP_K_MAX, seq // 4)
    k_top_s = min(TOP_K_MAX, (past_len + n_new) // 4)

    meta = jnp.broadcast_to(meta_tokens.astype(x_prompt.dtype)[None], (nb, n_meta, d))
    xp = jnp.concatenate([meta, x_prompt, jnp.zeros((nb, lp - n_tok, d), x_prompt.dtype)], axis=1)
    xp = xp.reshape(nb * lp, d)
    xs = x_sample.reshape(nseq * n_new, d)
    tabs_p = _rope_tables(jnp.arange(lp, dtype=I32))
    tabs_s = _rope_tables(past_len + jnp.arange(nseq * n_new, dtype=I32) % n_new)

    np_, pg = cache_k_a.shape[1], cache_k_a.shape[2]
    caches = (jnp.swapaxes(cache_k_idx, 2, 3),
              cache_k_a.reshape(depth, np_, pg * A_KV, HEAD_DIM),
              cache_v_a.reshape(depth, np_, pg * A_KV, HEAD_DIM),
              cache_k_b.reshape(depth, np_, pg * B_KV * 2, HEAD_DIM),
              cache_v_b.reshape(depth, np_, pg, B_KV, 2, HEAD_DIM).transpose(0, 1, 2, 4, 3, 5)
              .reshape(depth, np_, pg * B_KV * 2, HEAD_DIM))

    hp = _rmsnorm(xp, norm_g[0], BF16)
    hs = _rmsnorm(xs, norm_g[0], BF16)
    rows_p = [[] for _ in range(5)]
    rows_s = [[] for _ in range(5)]
    pad_t = lambda a, axis: jnp.pad(a, [(0, DEC_ROWS - n_new) if ax == axis else (0, 0) for ax in range(a.ndim)])
    for l in range(depth):
        lam_init = _lambda_init(l)
        weights = _split_weights(w_in[l])
        w_pa, w_pb, w_o = w_proj_a[l].astype(BF16), w_proj_b[l].astype(BF16), w_out[l].astype(BF16)
        last = l + 1 == depth
        g_next = final_norm_g if last else norm_g[l + 1]
        norm_dtype = F32 if last else BF16
        lq = lam_qk[l].astype(F32)
        g_sub = subln_g[l].reshape(1, B_VDIM).astype(F32)

        qrot, (kvb, nkb, nvb, nka, nva), (qe, kw, kk), zg = _project_all(hp, weights, tabs_p, nb, lp, n_tok, True)
        o_a = _dsa_prompt(qrot, qe, kw, kk, kvb, nb, lp, k_top_p)
        o_b = _diff_prompt(qrot, kvb, lq, g_sub, nb, lp, lam_init)
        xp, hp = _merge(xp, o_a, o_b, zg, w_pa, w_pb, w_o, g_next, norm_dtype)
        ki_p = kw.reshape(nb, lp, LANES)[:, :n_tok, :IDX_DIM]
        for i, a in enumerate((nka.reshape(nb, n_tok, A_KV, HEAD_DIM), nva.reshape(nb, n_tok, A_KV, HEAD_DIM), ki_p,
                               nkb.reshape(nb, n_tok, B_KV, 2, HEAD_DIM), nvb.reshape(nb, n_tok, B_KV, B_VDIM))):
            rows_p[i].append(a)

        ms = nseq * n_new
        qrot, (_, nkb, nvb, nka, nva), (qi, kw, _), zg = _project_all(hs, weights, tabs_s, 1, ms, ms, False)
        qa8 = pad_t(qrot[:, :A_WIDTH].reshape(nseq, n_new, A_KV, A_GROUP, HEAD_DIM).transpose(0, 2, 3, 1, 4), 3)
        qa8 = qa8.reshape(nseq, A_KV, A_GROUP * DEC_ROWS, HEAD_DIM)
        qb8 = qrot[:, A_WIDTH:].reshape(nseq, n_new, B_KV, B_GROUP, 2, HEAD_DIM).transpose(0, 2, 4, 3, 1, 5)
        qb8 = pad_t(qb8, 4).reshape(nseq, B_KV, 2, B_GROUP * DEC_ROWS, HEAD_DIM)
        qi8 = pad_t(qi.reshape(nseq, n_new, IDX_HEADS, IDX_DIM).transpose(0, 2, 1, 3), 2)
        qi8 = qi8.reshape(nseq, IDX_HEADS * DEC_ROWS, IDX_DIM)
        w8 = pad_t(kw[:, IDX_DIM:IDX_DIM + IDX_HEADS].reshape(nseq, n_new, IDX_HEADS), 1)
        ki_s = kw[:, :IDX_DIM].reshape(nseq, n_new, IDX_DIM)
        new_s = (nka.reshape(nseq, n_new, A_KV, HEAD_DIM), nva.reshape(nseq, n_new, A_KV, HEAD_DIM), ki_s,
                 nkb.reshape(nseq, n_new, B_KV, 2, HEAD_DIM), nvb.reshape(nseq, n_new, B_KV, B_VDIM))
        new_rows = [pad_t(new_s[j], 1) for j in (2, 0, 1, 3, 4)]
        new_rows[1:4] = [a.reshape(nseq, -1, HEAD_DIM) for a in new_rows[1:4]]
        new_rows[0] = jnp.pad(jnp.swapaxes(ki_s, 1, 2), ((0, 0), (0, 0), (0, pg - n_new)))
        new_rows[4] = (new_rows[4].reshape(nseq, DEC_ROWS, B_KV, 2, HEAD_DIM).transpose(0, 1, 3, 2, 4)
                       .reshape(nseq, -1, HEAD_DIM))
        o_a8, o_b8 = _decode_attend(page_table, qi8, w8, qa8, qb8, new_rows, lq, g_sub, caches, l, past_len,
                                    n_new, k_top_s, lam_init)
        o_a = o_a8[:, :n_new].reshape(ms, A_WIDTH)
        o_b = o_b8[:, :n_new].reshape(ms, B_WIDTH)
        xs, hs = _merge(xs, o_a, o_b, zg, w_pa, w_pb, w_o, g_next, norm_dtype)
        for i, a in enumerate(new_s):
            rows_s[i].append(a)

    y_prompt = hp.reshape(nb, lp, d)[:, n_meta:n_tok]
    y_sample = hs.reshape(nseq, n_new, d)
    return (y_prompt, y_sample) + tuple(jnp.stack(r) for r in rows_p) + tuple(jnp.stack(r) for r in rows_s)
```

```python
import functools
import math

import jax
import jax.numpy as jnp
import numpy as np
from jax import lax
from jax.experimental import pallas as pl
from jax.experimental.pallas import tpu as pltpu

HEAD_DIM = 128
A_HEADS = 8
A_KV = 2
A_GROUP = A_HEADS // A_KV
A_WIDTH = A_HEADS * HEAD_DIM
IDX_HEADS = 16
IDX_DIM = 64
B_HEADS = 4
B_KV = 2
B_GROUP = B_HEADS // B_KV
B_VDIM = 2 * HEAD_DIM
B_WIDTH = B_HEADS * B_VDIM
TOP_K_MAX = 256
ROPE_THETA = 10000.0
EPS = 1e-6

LANES = 128
ROW_TILE = 256
GATE_ROW_TILE = 512
GATE_COL_TILE = 2048
KEY_CHUNK = 256
DSA_Q_TILE = 256
SEARCH_ROWS = 128
DIFF_Q_TILE = 256
MERGE_TILE = 256
DEC_ROWS = 8
VMEM_LIMIT = 48 * 1024 * 1024

LOG2_E = 1.4426950408889634
NEG_BIG = -1e30
INT_MIN = -2 ** 31
KEY_NEG_INF = -2139095041
KEY_POS_INF = 2139095040

F32 = jnp.float32
BF16 = jnp.bfloat16
I32 = jnp.int32

_NT = (((1,), (1,)), ((), ()))


def _params(*sem):
    return pltpu.CompilerParams(dimension_semantics=sem, vmem_limit_bytes=VMEM_LIMIT)


def _lambda_init(layer):
    return 0.8 - 0.6 * math.exp(-0.3 * layer)


def _float_key(x):
    b = lax.bitcast_convert_type(x, I32)
    return b ^ ((b >> 31) & jnp.int32(0x7FFFFFFF))


def _rmsnorm_kernel(x_ref, g_ref, o_ref):
    x = x_ref[...]
    ms = jnp.mean(x * x, axis=-1, keepdims=True)
    o_ref[...] = (x * lax.rsqrt(ms + EPS) * g_ref[...]).astype(o_ref.dtype)


def _rmsnorm(x, g, out_dtype):
    m, d = x.shape
    tm = min(ROW_TILE, m)
    return pl.pallas_call(
        _rmsnorm_kernel,
        grid=(m // tm,),
        in_specs=[pl.BlockSpec((tm, d), lambda i: (i, 0)),
                  pl.BlockSpec((1, d), lambda i: (0, 0))],
        out_specs=pl.BlockSpec((tm, d), lambda i: (i, 0)),
        out_shape=jax.ShapeDtypeStruct((m, d), out_dtype),
        compiler_params=_params("arbitrary"),
        name="rmsnorm",
    )(x, g.reshape(1, d).astype(F32))


def _rope128(a, c, s):
    return a * c + pltpu.roll(a, HEAD_DIM // 2, 1) * s


def _rope64(a, c, s_lo, s_hi):
    return a * c + pltpu.roll(a, LANES - IDX_DIM // 2, 1) * s_lo + pltpu.roll(a, IDX_DIM // 2, 1) * s_hi


def _proj_q_kernel(h_ref, w_ref, c_ref, s_ref, o_ref):
    acc = jnp.dot(h_ref[...], w_ref[...], preferred_element_type=F32)
    c, s = c_ref[...], s_ref[...]
    for j in range(acc.shape[1] // LANES):
        sl = slice(j * LANES, (j + 1) * LANES)
        o_ref[:, sl] = _rope128(acc[:, sl], c, s).astype(o_ref.dtype)


def _proj_kv_kernel(h_ref, w_ref, c_ref, s_ref, kv_ref, kb_ref, vb_ref, ka_ref, va_ref):
    acc = jnp.dot(h_ref[0], w_ref[...], preferred_element_type=F32)
    c, s = c_ref[...], s_ref[...]
    outs = ((kb_ref, 0, 4, True), (vb_ref, 4, 4, False), (ka_ref, 8, 2, True), (va_ref, 10, 2, False))
    for ref, start, count, rot in outs:
        for j in range(count):
            a = acc[:, (start + j) * LANES:(start + j + 1) * LANES]
            if rot:
                a = _rope128(a, c, s)
            ref[0, :, j * LANES:(j + 1) * LANES] = a
            kv_ref[0, :, (start + j) * LANES:(start + j + 1) * LANES] = a.astype(kv_ref.dtype)


def _proj_idx_kernel(h_ref, w_ref, c_ref, slo_ref, shi_ref, ck_ref, klo_ref, khi_ref,
                     q_ref, kw_ref, kk_ref, *, expand):
    acc = jnp.dot(h_ref[...], w_ref[...], preferred_element_type=F32)
    c, slo, shi = c_ref[...], slo_ref[...], shi_ref[...]
    lane = lax.broadcasted_iota(I32, (acc.shape[0], LANES), 1)
    low = lane < IDX_DIM
    for j in range(IDX_HEADS // 2):
        a = _rope64(acc[:, j * LANES:(j + 1) * LANES], c, slo, shi)
        if expand:
            q_ref[:, (2 * j) * LANES:(2 * j + 1) * LANES] = jnp.where(low, a, 0.0).astype(q_ref.dtype)
            q_ref[:, (2 * j + 1) * LANES:(2 * j + 2) * LANES] = jnp.where(low, 0.0, a).astype(q_ref.dtype)
        else:
            q_ref[:, j * LANES:(j + 1) * LANES] = a.astype(q_ref.dtype)
    kw = _rope64(acc[:, IDX_HEADS * IDX_DIM:], ck_ref[...], klo_ref[...], khi_ref[...])
    kw_ref[...] = kw
    kk_ref[...] = jnp.where(low, kw, pltpu.roll(kw, IDX_DIM, 1)).astype(kk_ref.dtype)


def _proj_plain_kernel(h_ref, w_ref, o_ref):
    o_ref[...] = jnp.dot(h_ref[...], w_ref[...], preferred_element_type=F32).astype(o_ref.dtype)


def _row_spec(tm, width, ntab=None):
    if ntab is None:
        return pl.BlockSpec((tm, width), lambda j, i: (i, 0))
    return pl.BlockSpec((tm, width), lambda j, i: (i % ntab, 0))


def _proj_q(h, w, tabs):
    m, d = h.shape
    n = w.shape[1]
    tm = min(ROW_TILE, m)
    ntab = tabs[0].shape[0] // tm
    return pl.pallas_call(
        _proj_q_kernel,
        grid=(1, m // tm),
        in_specs=[_row_spec(tm, d), pl.BlockSpec((d, n), lambda j, i: (0, 0)),
                  _row_spec(tm, LANES, ntab), _row_spec(tm, LANES, ntab)],
        out_specs=_row_spec(tm, n),
        out_shape=jax.ShapeDtypeStruct((m, n), BF16),
        compiler_params=_params("arbitrary", "arbitrary"),
        name="proj_q",
    )(h, w, tabs[0], tabs[1])


def _proj_kv(h3, w, tabs, n_keep):
    nb, lp, d = h3.shape
    n = w.shape[1]
    tm = min(DIFF_Q_TILE, lp)
    row = lambda width: pl.BlockSpec((1, tm, width), lambda b, i: (b, i, 0))
    tab = pl.BlockSpec((tm, LANES), lambda b, i: (i, 0))
    f32_out = lambda width: jax.ShapeDtypeStruct((nb, n_keep, width), F32)
    return pl.pallas_call(
        _proj_kv_kernel,
        grid=(nb, lp // tm),
        in_specs=[row(d), pl.BlockSpec((d, n), lambda b, i: (0, 0)), tab, tab],
        out_specs=[row(n), row(4 * LANES), row(4 * LANES), row(2 * LANES), row(2 * LANES)],
        out_shape=[jax.ShapeDtypeStruct((nb, lp, n), BF16),
                   f32_out(4 * LANES), f32_out(4 * LANES), f32_out(2 * LANES), f32_out(2 * LANES)],
        compiler_params=_params("arbitrary", "arbitrary"),
        name="proj_kv",
    )(h3, w, tabs[0], tabs[1])


def _proj_idx(h, w, tabs, expand):
    m, d = h.shape
    n = w.shape[1]
    tm = min(ROW_TILE, m)
    ntab = tabs[0].shape[0] // tm
    qw = IDX_HEADS * (LANES if expand else IDX_DIM)
    return pl.pallas_call(
        functools.partial(_proj_idx_kernel, expand=expand),
        grid=(1, m // tm),
        in_specs=[_row_spec(tm, d), pl.BlockSpec((d, n), lambda j, i: (0, 0))]
                 + [_row_spec(tm, LANES, ntab)] * 6,
        out_specs=[_row_spec(tm, qw), _row_spec(tm, LANES), _row_spec(tm, LANES)],
        out_shape=[jax.ShapeDtypeStruct((m, qw), BF16),
                   jax.ShapeDtypeStruct((m, LANES), F32),
                   jax.ShapeDtypeStruct((m, LANES), BF16)],
        compiler_params=_params("arbitrary", "arbitrary"),
        name="proj_idx",
    )(h, w, *tabs)


def _proj_plain(h, w, tn):
    m, d = h.shape
    n = w.shape[1]
    tm = GATE_ROW_TILE if m % GATE_ROW_TILE == 0 else min(ROW_TILE, m)
    return pl.pallas_call(
        _proj_plain_kernel,
        grid=(n // tn, m // tm),
        in_specs=[_row_spec(tm, d), pl.BlockSpec((d, tn), lambda j, i: (0, j))],
        out_specs=pl.BlockSpec((tm, tn), lambda j, i: (i, j)),
        out_shape=jax.ShapeDtypeStruct((m, n), F32),
        compiler_params=_params("arbitrary", "arbitrary"),
        name="proj_gate",
    )(h, w)


def _search_threshold(count_ge, rows, k_top):
    zero = jnp.zeros((rows, 1), I32)
    t = jnp.where(count_ge(zero) >= k_top, zero, jnp.int32(INT_MIN))

    def bit_body(bi, t):
        cand = t | lax.shift_left(jnp.int32(1), 30 - bi)
        return jnp.where(count_ge(cand) >= k_top, cand, t)
    return lax.fori_loop(0, 31, bit_body, t)


def _strict_upper(n):
    r = lax.broadcasted_iota(I32, (n, n), 0)
    c = lax.broadcasted_iota(I32, (n, n), 1)
    return jnp.where(r < c, 1.0, 0.0).astype(BF16)


def _dsa_kernel(qa_ref, qe_ref, kw_ref, kk_ref, ka_ref, va_ref, o_ref, sk_ref, *, k_top):
    tq = qa_ref.shape[0]
    kc = KEY_CHUNK
    hr = SEARCH_ROWS
    i = pl.program_id(1)
    q0 = i * tq
    nkc = (q0 + tq + kc - 1) // kc
    col_iota = lax.broadcasted_iota(I32, (hr, kc), 1)
    row_iota = lax.broadcasted_iota(I32, (hr, kc), 0)

    thr_slabs = []
    for r0 in range(0, tq, hr):
        w_idx = kw_ref[r0:r0 + hr, IDX_DIM:IDX_DIM + IDX_HEADS] * (IDX_DIM ** -0.5)
        w_cols = [jnp.broadcast_to(w_idx[:, h:h + 1], (hr, LANES)) for h in range(IDX_HEADS)]
        q_stack = jnp.concatenate([qe_ref[r0:r0 + hr, h * LANES:(h + 1) * LANES] for h in range(IDX_HEADS)],
                                  axis=0)

        def score_chunk(c, q_stack=q_stack, w_cols=w_cols):
            k0 = pl.multiple_of(c * kc, kc)
            d = lax.dot_general(q_stack, kk_ref[pl.ds(k0, kc), :], _NT, preferred_element_type=F32)
            parts = []
            for j in range(kc // LANES):
                acc = jnp.zeros((hr, LANES), F32)
                for h in range(IDX_HEADS):
                    acc = acc + jnp.maximum(d[h * hr:(h + 1) * hr, j * LANES:(j + 1) * LANES], 0.0) * w_cols[h]
                parts.append(acc)
            return _float_key(jnp.concatenate(parts, axis=1)), k0

        def p1_body(c, carry, r0=r0, score_chunk=score_chunk):
            key, _ = score_chunk(c)
            sk_ref[c, r0:r0 + hr, :] = key
            return carry
        lax.fori_loop(0, nkc - 1, p1_body, 0)
        key, k0 = score_chunk(nkc - 1)
        sk_ref[nkc - 1, r0:r0 + hr, :] = jnp.where(k0 + col_iota <= q0 + r0 + row_iota, key, jnp.int32(INT_MIN))

        def count_cmp(cand, strict, r0=r0):
            cand_b = jnp.broadcast_to(cand, (hr, LANES))
            def body(c, acc):
                for j in range(kc // LANES):
                    kj = sk_ref[c, r0:r0 + hr, j * LANES:(j + 1) * LANES]
                    hit = (kj > cand_b) if strict else (kj >= cand_b)
                    acc = acc + jnp.where(hit, 1.0, 0.0)
                return acc
            acc = lax.fori_loop(0, nkc, body, jnp.zeros((hr, LANES), F32))
            return jnp.sum(acc, axis=1, keepdims=True)

        thr = _search_threshold(lambda cand: count_cmp(cand, False), hr, float(k_top))
        c_ge = count_cmp(thr, False)
        c_gt = count_cmp(thr, True)
        need = float(k_top) - c_gt
        tie_rows = jnp.where((c_ge > float(k_top)) & (thr > jnp.int32(INT_MIN)), 1.0, 0.0)

        @pl.when(jnp.max(tie_rows) > 0.0)
        def _(r0=r0, thr=thr, need=need):
            tri = _strict_upper(kc)
            def body(c, run):
                k = sk_ref[c, r0:r0 + hr, :]
                eq = k == thr
                eqf = jnp.where(eq, 1.0, 0.0)
                before = jnp.dot(eqf.astype(BF16), tri, preferred_element_type=F32) + run
                sk_ref[c, r0:r0 + hr, :] = jnp.where(eq & (before >= need), jnp.int32(INT_MIN), k)
                return run + jnp.sum(eqf, axis=1, keepdims=True)
            lax.fori_loop(0, nkc, body, jnp.zeros((hr, 1), F32))

        thr_slabs.append(thr)
    thr = jnp.concatenate(thr_slabs, axis=0)

    scale2 = HEAD_DIM ** -0.5 * LOG2_E

    def attend(c0, width, carry):
        k0 = pl.multiple_of(c0 * kc, kc)
        key = jnp.concatenate([sk_ref[c0 + u] for u in range(width)], axis=1)
        sel = (key >= thr) & (key > jnp.int32(KEY_NEG_INF)) & (key < jnp.int32(KEY_POS_INF))
        bias = jnp.where(sel, 0.0, NEG_BIG)
        out = []
        for g in range(A_KV):
            qg = jnp.concatenate([qa_ref[:, (g * A_GROUP + r) * HEAD_DIM:(g * A_GROUP + r + 1) * HEAD_DIM]
                                  for r in range(A_GROUP)], axis=0)
            kch = ka_ref[0, pl.ds(k0, width * kc), g * HEAD_DIM:(g + 1) * HEAD_DIM]
            vch = va_ref[0, pl.ds(k0, width * kc), g * HEAD_DIM:(g + 1) * HEAD_DIM]
            s = lax.dot_general(qg, kch, _NT, preferred_element_type=F32)
            ps, stats = [], []
            for r in range(A_GROUP):
                m, l, _ = carry[g * A_GROUP + r]
                sr = s[r * tq:(r + 1) * tq] * scale2 + bias
                m_new = jnp.maximum(m, jnp.max(sr, axis=1, keepdims=True))
                p = jnp.exp2(sr - m_new)
                alpha = jnp.exp2(m - m_new)
                stats.append((m_new, alpha * l + jnp.sum(p, axis=1, keepdims=True), alpha))
                ps.append(p.astype(BF16))
            pv = jnp.dot(jnp.concatenate(ps, axis=0), vch, preferred_element_type=F32)
            for r in range(A_GROUP):
                m_new, l_new, alpha = stats[r]
                out.append((m_new, l_new, alpha * carry[g * A_GROUP + r][2] + pv[r * tq:(r + 1) * tq]))
        return tuple(out)

    init = tuple((jnp.full((tq, 1), NEG_BIG, F32), jnp.zeros((tq, 1), F32),
                  jnp.zeros((tq, HEAD_DIM), F32)) for _ in range(A_HEADS))
    n_pair = nkc // 2
    fin = lax.fori_loop(0, n_pair, lambda j, carry: attend(2 * j, 2, carry), init)
    fin = lax.fori_loop(2 * n_pair, nkc, lambda c, carry: attend(c, 1, carry), fin)
    for hh in range(A_HEADS):
        o_ref[:, hh * HEAD_DIM:(hh + 1) * HEAD_DIM] = fin[hh][2] / fin[hh][1]


def _dsa_prompt(qrot, qe, kw, kk, kvb, nb, lp, k_top):
    m = qrot.shape[0]
    tq = DSA_Q_TILE
    nq = lp // tq
    qrow = lambda width: pl.BlockSpec((tq, width), lambda b, i: (b * nq + i, 0))
    return pl.pallas_call(
        functools.partial(_dsa_kernel, k_top=k_top),
        grid=(nb, nq),
        in_specs=[qrow(A_WIDTH), qrow(IDX_HEADS * LANES), qrow(LANES),
                  pl.BlockSpec((lp, LANES), lambda b, i: (b, 0)),
                  pl.BlockSpec((1, lp, A_KV * HEAD_DIM), lambda b, i: (b, 0, 4)),
                  pl.BlockSpec((1, lp, A_KV * HEAD_DIM), lambda b, i: (b, 0, 5))],
        out_specs=qrow(A_WIDTH),
        out_shape=jax.ShapeDtypeStruct((m, A_WIDTH), F32),
        scratch_shapes=[pltpu.VMEM((lp // KEY_CHUNK, tq, KEY_CHUNK), I32)],
        compiler_params=_params("arbitrary", "arbitrary"),
        name="dsa_prompt",
    )(qrot, qe, kw, kk, kvb, kvb)


def _lambda_full(lq_ref, lam_init):
    lq = lq_ref[...]
    a = jnp.sum(lq[0:1] * lq[1:2], axis=1, keepdims=True)
    b = jnp.sum(lq[2:3] * lq[3:4], axis=1, keepdims=True)
    return jnp.exp(a) - jnp.exp(b) + lam_init


def _subln(o, g, lam_init):
    ms = jnp.mean(o * o, axis=-1, keepdims=True)
    return (o * lax.rsqrt(ms + EPS) * g) * (1.0 - lam_init)


def _diff_kernel(qb_ref, kb_ref, vb_ref, lq_ref, g_ref, o_ref, *, lam_init):
    tq = qb_ref.shape[0]
    kc = KEY_CHUNK
    i = pl.program_id(1)
    q0 = i * tq
    n_full = q0 // kc
    n_all = (q0 + tq + kc - 1) // kc
    lam = _lambda_full(lq_ref, lam_init)
    scale2 = HEAD_DIM ** -0.5 * LOG2_E
    rows = B_GROUP * tq
    row_pos = q0 + lax.broadcasted_iota(I32, (tq, kc), 0)
    col_iota = lax.broadcasted_iota(I32, (tq, kc), 1)
    row_pos = jnp.concatenate([row_pos] * B_GROUP, axis=0)
    col_iota = jnp.concatenate([col_iota] * B_GROUP, axis=0)

    combos = [(g, c2) for g in range(B_KV) for c2 in range(2)]

    def step(c0, carry, width, masked):
        k0 = pl.multiple_of(c0 * kc, kc)
        out = []
        for idx, (g, c2) in enumerate(combos):
            m, l, acc = carry[idx]
            qgc = jnp.concatenate(
                [qb_ref[:, ((g * B_GROUP + r) * 2 + c2) * HEAD_DIM:((g * B_GROUP + r) * 2 + c2 + 1) * HEAD_DIM]
                 for r in range(B_GROUP)], axis=0)
            kcol = (g * 2 + c2) * HEAD_DIM
            kch = kb_ref[0, pl.ds(k0, width * kc), kcol:kcol + HEAD_DIM]
            vch = vb_ref[0, pl.ds(k0, width * kc), g * B_VDIM:(g + 1) * B_VDIM]
            s = lax.dot_general(qgc, kch, _NT, preferred_element_type=F32) * scale2
            if masked:
                s = jnp.where(k0 + col_iota <= row_pos, s, NEG_BIG)
            m_new = jnp.maximum(m, jnp.max(s, axis=1, keepdims=True))
            p = jnp.exp2(s - m_new)
            alpha = jnp.exp2(m - m_new)
            l = alpha * l + jnp.sum(p, axis=1, keepdims=True)
            acc = alpha * acc + jnp.dot(p.astype(BF16), vch, preferred_element_type=F32)
            out.append((m_new, l, acc))
        return tuple(out)

    init = tuple((jnp.full((rows, 1), NEG_BIG, F32), jnp.zeros((rows, 1), F32),
                  jnp.zeros((rows, B_VDIM), F32)) for _ in combos)
    n_pair = n_full // 2
    carry = lax.fori_loop(0, n_pair, lambda j, cr: step(2 * j, cr, 2, False), init)
    carry = lax.fori_loop(2 * n_pair, n_full, lambda c, cr: step(c, cr, 1, False), carry)
    carry = lax.fori_loop(n_full, n_all, lambda c, cr: step(c, cr, 1, True), carry)
    for g in range(B_KV):
        o0, o1 = (carry[g * 2 + c2][2] / carry[g * 2 + c2][1] for c2 in range(2))
        o = _subln(o0 - lam * o1, g_ref[...], lam_init)
        for r in range(B_GROUP):
            hh = g * B_GROUP + r
            o_ref[:, hh * B_VDIM:(hh + 1) * B_VDIM] = o[r * tq:(r + 1) * tq]


def _diff_prompt(qrot, kvb, lam_qk, g_sub, nb, lp, lam_init):
    m = qrot.shape[0]
    tq = DIFF_Q_TILE
    nq = lp // tq
    return pl.pallas_call(
        functools.partial(_diff_kernel, lam_init=lam_init),
        grid=(nb, nq),
        in_specs=[pl.BlockSpec((tq, B_HEADS * 2 * HEAD_DIM), lambda b, i: (b * nq + i, 1)),
                  pl.BlockSpec((1, lp, B_KV * 2 * HEAD_DIM), lambda b, i: (b, 0, 0)),
                  pl.BlockSpec((1, lp, B_KV * B_VDIM), lambda b, i: (b, 0, 1)),
                  pl.BlockSpec((4, HEAD_DIM), lambda b, i: (0, 0)),
                  pl.BlockSpec((1, B_VDIM), lambda b, i: (0, 0))],
        out_specs=pl.BlockSpec((tq, B_WIDTH), lambda b, i: (b * nq + i, 0)),
        out_shape=jax.ShapeDtypeStruct((m, B_WIDTH), F32),
        compiler_params=_params("arbitrary", "arbitrary"),
        name="diff_prompt",
    )(qrot, kvb, kvb, lam_qk, g_sub)


def _sigmoid(x):
    return 1.0 / (1.0 + jnp.exp(-x))


def _merge_kernel(x_ref, oa_ref, ob_ref, za_ref, zb_ref, ga_ref, gb_ref, wa_ref, wb_ref, wo_ref, g_ref,
                  x_out_ref, n_out_ref):
    za, zb = za_ref[...], zb_ref[...]
    ya = (oa_ref[...] * (za * _sigmoid(za))).astype(BF16)
    yb = (ob_ref[...] * (zb * _sigmoid(zb))).astype(BF16)
    branch_a = jnp.dot(ya, wa_ref[...], preferred_element_type=F32)
    branch_b = jnp.dot(yb, wb_ref[...], preferred_element_type=F32)
    mix = _sigmoid(ga_ref[...]) * branch_a + _sigmoid(gb_ref[...]) * branch_b
    x_new = x_ref[...] + jnp.dot(mix.astype(BF16), wo_ref[...], preferred_element_type=F32)
    x_out_ref[...] = x_new
    ms = jnp.mean(x_new * x_new, axis=-1, keepdims=True)
    n_out_ref[...] = (x_new * lax.rsqrt(ms + EPS) * g_ref[...]).astype(n_out_ref.dtype)


def _merge(x, o_a, o_b, zg, w_pa, w_pb, w_o, g_next, norm_dtype):
    m, d = x.shape
    tm = min(MERGE_TILE, m)
    row = lambda width, col: pl.BlockSpec((tm, width), lambda i: (i, col))
    full = lambda a: pl.BlockSpec(a.shape, lambda i: (0, 0), pipeline_mode=pl.Buffered(1))
    return pl.pallas_call(
        _merge_kernel,
        grid=(m // tm,),
        in_specs=[row(d, 0), row(A_WIDTH, 0), row(B_WIDTH, 0),
                  row(A_WIDTH, 0), row(B_WIDTH, 1), row(d, 1), row(d, 2),
                  full(w_pa), full(w_pb), full(w_o), pl.BlockSpec((1, d), lambda i: (0, 0))],
        out_specs=[row(d, 0), row(d, 0)],
        out_shape=[jax.ShapeDtypeStruct((m, d), F32), jax.ShapeDtypeStruct((m, d), norm_dtype)],
        compiler_params=_params("arbitrary"),
        name="merge",
    )(x, o_a, o_b, zg, zg, zg, zg, w_pa, w_pb, w_o, g_next.reshape(1, d).astype(F32))


def _decode_kernel(pt_ref, qi_ref, w_ref, qa_ref, qb_ref, kin_ref, kan_ref, van_ref, kbn_ref, vbn_ref,
                   lq_ref, g_ref, cki_ref, cka_ref, cva_ref, ckb_ref, cvb_ref,
                   oa_ref, ob_ref, bki, bka, bva, bkb, bvb, sems, *, layer, past_len, n_new, k_top, lam_init):
    b = pl.program_id(0)
    nb = pl.num_programs(0)
    n_pages = bki.shape[1] - 1
    page = bki.shape[3]
    s_all = (n_pages + 1) * page
    pairs = ((cki_ref, bki), (cka_ref, bka), (cva_ref, bva), (ckb_ref, bkb), (cvb_ref, bvb))

    def page_copy(src, dst, which, seq, slot, p):
        return pltpu.make_async_copy(src.at[layer, pt_ref[seq, p]], dst.at[slot, p], sems.at[slot, which])

    def start_all(seq, slot):
        for p in range(n_pages):
            for which, (src, dst) in enumerate(pairs):
                page_copy(src, dst, which, seq, slot, p).start(priority=p % 2)

    @pl.when(b == 0)
    def _():
        start_all(0, 0)

    slot = lax.rem(b, 2)

    @pl.when(b + 1 < nb)
    def _():
        start_all(b + 1, 1 - slot)

    for buf, new in ((bki, kin_ref), (bka, kan_ref), (bva, van_ref), (bkb, kbn_ref), (bvb, vbn_ref)):
        buf[slot, n_pages] = jnp.zeros(buf.shape[2:], F32)
        buf[slot, n_pages, 0:new.shape[1]] = new[0]

    for p in range(n_pages):
        for which, (src, dst) in enumerate(pairs):
            page_copy(src, dst, which, b, slot, p).wait()

    rows = DEC_ROWS
    col = lax.broadcasted_iota(I32, (rows, s_all), 1)
    tok = jnp.minimum(lax.broadcasted_iota(I32, (rows, s_all), 0), n_new - 1)
    causal = (col <= past_len + tok) & (col < past_len + n_new)

    kidx_t = jnp.concatenate([bki[slot, p] for p in range(n_pages + 1)], axis=1).astype(BF16)
    d = jnp.dot(qi_ref[0], kidx_t, preferred_element_type=F32)
    w_idx = w_ref[0] * (IDX_DIM ** -0.5)
    acc = jnp.zeros((rows, s_all), F32)
    for h in range(IDX_HEADS):
        acc = acc + jnp.maximum(d[h * rows:(h + 1) * rows], 0.0) * w_idx[:, h:h + 1]
    key = jnp.where(causal, _float_key(acc), jnp.int32(INT_MIN))

    def count_ge(cand):
        return jnp.sum(jnp.where(key >= cand, 1.0, 0.0), axis=1, keepdims=True)

    thr = _search_threshold(count_ge, rows, float(k_top))
    c_ge = count_ge(thr)
    c_gt = jnp.sum(jnp.where(key > thr, 1.0, 0.0), axis=1, keepdims=True)
    need = float(k_top) - c_gt
    sel = (key >= thr) & (key > jnp.int32(KEY_NEG_INF)) & (key < jnp.int32(KEY_POS_INF))
    sel_bias = jnp.where(sel, 0.0, NEG_BIG)

    def drop_ties(sel_bias):
        eq = key == thr
        tri = _strict_upper(LANES)
        run = jnp.zeros((rows, 1), F32)
        drops = []
        for j in range(s_all // LANES):
            eqj = jnp.where(eq[:, j * LANES:(j + 1) * LANES], 1.0, 0.0)
            before = jnp.dot(eqj.astype(BF16), tri, preferred_element_type=F32) + run
            drops.append(before >= need)
            run = run + jnp.sum(eqj, axis=1, keepdims=True)
        drop = eq & jnp.concatenate(drops, axis=1) & (c_ge > float(k_top))
        return jnp.where(drop, NEG_BIG, sel_bias)

    tie_rows = jnp.where((c_ge > float(k_top)) & (thr > jnp.int32(INT_MIN)), 1.0, 0.0)
    sel_bias = lax.cond(jnp.max(tie_rows) > 0.0, drop_ties, lambda x: x, sel_bias)
    causal_bias = jnp.where(causal, 0.0, NEG_BIG)
    scale = HEAD_DIM ** -0.5

    def softmax_pv(s, v):
        m = jnp.max(s, axis=1, keepdims=True)
        p = jnp.exp(s - m)
        l = jnp.sum(p, axis=1, keepdims=True)
        return jnp.dot(p.astype(v.dtype), v, preferred_element_type=F32) / l

    bias_a = jnp.concatenate([sel_bias] * A_GROUP, axis=0)
    for g in range(A_KV):
        kg = bka[slot, :, pl.ds(g, page, stride=A_KV), :].reshape(s_all, HEAD_DIM).astype(BF16)
        vg = bva[slot, :, pl.ds(g, page, stride=A_KV), :].reshape(s_all, HEAD_DIM).astype(BF16)
        s = lax.dot_general(qa_ref[0, g], kg, _NT, preferred_element_type=F32) * scale + bias_a
        o = softmax_pv(s, vg)
        for r in range(A_GROUP):
            hh = g * A_GROUP + r
            oa_ref[0, :, hh * HEAD_DIM:(hh + 1) * HEAD_DIM] = o[r * rows:(r + 1) * rows]

    lam = _lambda_full(lq_ref, lam_init)
    bias_b = jnp.concatenate([causal_bias] * B_GROUP, axis=0)
    for g in range(B_KV):
        vg = jnp.concatenate(
            [bvb[slot, :, pl.ds(hf * B_KV + g, page, stride=2 * B_KV), :].reshape(s_all, HEAD_DIM)
             for hf in range(2)], axis=1).astype(BF16)
        outs = []
        for c2 in range(2):
            kgc = bkb[slot, :, pl.ds(g * 2 + c2, page, stride=2 * B_KV), :].reshape(s_all, HEAD_DIM).astype(BF16)
            s = lax.dot_general(qb_ref[0, g, c2], kgc, _NT, preferred_element_type=F32) * scale + bias_b
            outs.append(softmax_pv(s, vg))
        o = _subln(outs[0] - lam * outs[1], g_ref[...], lam_init)
        for r in range(B_GROUP):
            hh = g * B_GROUP + r
            ob_ref[0, :, hh * B_VDIM:(hh + 1) * B_VDIM] = o[r * rows:(r + 1) * rows]


def _decode_attend(page_table, qi8, w8, qa8, qb8, new_rows, lam_qk, g_sub, caches, layer, past_len, n_new,
                   k_top, lam_init):
    nseq, n_pages = page_table.shape
    page = caches[0].shape[2]
    seq_spec = lambda a: pl.BlockSpec((1,) + a.shape[1:], lambda b, pt: (b,) + (0,) * (a.ndim - 1))
    const = lambda a: pl.BlockSpec(a.shape, lambda b, pt: (0,) * a.ndim)
    any_spec = pl.BlockSpec(memory_space=pl.ANY)
    per_seq = [qi8, w8, qa8, qb8] + list(new_rows)
    buf = lambda c: pltpu.VMEM((2, n_pages + 1) + c.shape[2:], F32)
    grid_spec = pltpu.PrefetchScalarGridSpec(
        num_scalar_prefetch=1,
        grid=(nseq,),
        in_specs=[seq_spec(a) for a in per_seq] + [const(lam_qk), const(g_sub)] + [any_spec] * 5,
        out_specs=[pl.BlockSpec((1, DEC_ROWS, A_WIDTH), lambda b, pt: (b, 0, 0)),
                   pl.BlockSpec((1, DEC_ROWS, B_WIDTH), lambda b, pt: (b, 0, 0))],
        scratch_shapes=[buf(c) for c in caches] + [pltpu.SemaphoreType.DMA((2, 5))],
    )
    return pl.pallas_call(
        functools.partial(_decode_kernel, layer=layer, past_len=past_len, n_new=n_new, k_top=k_top,
                          lam_init=lam_init),
        grid_spec=grid_spec,
        out_shape=[jax.ShapeDtypeStruct((nseq, DEC_ROWS, A_WIDTH), F32),
                   jax.ShapeDtypeStruct((nseq, DEC_ROWS, B_WIDTH), F32)],
        compiler_params=_params("arbitrary"),
        name="decode_attend",
    )(page_table, *per_seq, lam_qk, g_sub, *caches)


def _rope_tables(pos):
    def cs(dim):
        half = dim // 2
        inv = ROPE_THETA ** (-jnp.arange(half, dtype=F32) / half)
        ang = pos.astype(F32)[:, None] * inv[None, :]
        return jnp.cos(ang), jnp.sin(ang)
    cos, sin = cs(HEAD_DIM)
    c128 = jnp.concatenate([cos, cos], axis=1)
    s128 = jnp.concatenate([-sin, sin], axis=1)
    cos, sin = cs(IDX_DIM)
    zero = jnp.zeros_like(sin)
    c64 = jnp.concatenate([cos, cos, cos, cos], axis=1)
    lo64 = jnp.concatenate([-sin, zero, -sin, zero], axis=1)
    hi64 = jnp.concatenate([zero, sin, zero, sin], axis=1)
    n = pos.shape[0]
    tail = jnp.concatenate([jnp.full((n, IDX_HEADS), IDX_HEADS ** -0.5, F32),
                            jnp.zeros((n, LANES - IDX_DIM - IDX_HEADS), F32)], axis=1)
    ck = jnp.concatenate([cos, cos, tail], axis=1)
    zero_tail = jnp.zeros((n, LANES - IDX_DIM), F32)
    klo = jnp.concatenate([-sin, zero, zero_tail], axis=1)
    khi = jnp.concatenate([zero, sin, zero_tail], axis=1)
    return (c128, s128), (c64, lo64, hi64, ck, klo, khi)


def _split_weights(w):
    sizes = (A_WIDTH, A_KV * HEAD_DIM, A_KV * HEAD_DIM, A_WIDTH, IDX_HEADS * IDX_DIM, IDX_DIM, IDX_HEADS,
             B_HEADS * 2 * HEAD_DIM, B_KV * 2 * HEAD_DIM, B_KV * B_VDIM, B_WIDTH, w.shape[0], w.shape[0])
    pts = np.cumsum(sizes)[:-1]
    q_a, k_a, v_a, z_a, qi, ki, wi, q_b, k_b, v_b, z_b, g_a, g_b = jnp.split(w, pts, axis=1)
    pad = jnp.zeros((w.shape[0], LANES - IDX_DIM - IDX_HEADS), w.dtype)
    cat = lambda parts: jnp.concatenate(parts, axis=1).astype(BF16)
    return (cat([q_a, q_b]), cat([k_b, v_b, k_a, v_a]), cat([qi, ki, wi, pad]), cat([z_a, z_b, g_a, g_b]))


def _project_all(h, weights, tabs, nb, lp, n_keep, expand):
    w_q, w_kv, w_idx, w_zg = weights
    qrot = _proj_q(h, w_q, tabs[0])
    kv = _proj_kv(h.reshape(nb, lp, h.shape[1]), w_kv, tabs[0], n_keep)
    idx = _proj_idx(h, w_idx, tabs[1], expand)
    zg = _proj_plain(h, w_zg, GATE_COL_TILE)
    return qrot, kv, idx, zg


def kernel(x_prompt, x_sample, cache_k_a, cache_v_a, cache_k_idx, cache_k_b, cache_v_b, page_table,
           meta_tokens, norm_g, w_in, lam_qk, subln_g, w_proj_a, w_proj_b, w_out, final_norm_g):
    nb, seq, d = x_prompt.shape
    nseq, n_new, _ = x_sample.shape
    depth = norm_g.shape[0]
    n_meta = meta_tokens.shape[0]
    n_pages = page_table.shape[1]
    page = cache_k_a.shape[2]
    past_len = n_pages * page
    n_tok = n_meta + seq
    lp = -(-n_tok // ROW_TILE) * ROW_TILE
    k_top_p = min(TOP_K_MAX, seq // 4)
    k_top_s = min(TOP_K_MAX, (past_len + n_new) // 4)

    meta = jnp.broadcast_to(meta_tokens.astype(x_prompt.dtype)[None], (nb, n_meta, d))
    xp = jnp.concatenate([meta, x_prompt, jnp.zeros((nb, lp - n_tok, d), x_prompt.dtype)], axis=1)
    xp = xp.reshape(nb * lp, d)
    xs = x_sample.reshape(nseq * n_new, d)
    tabs_p = _rope_tables(jnp.arange(lp, dtype=I32))
    tabs_s = _rope_tables(past_len + jnp.arange(nseq * n_new, dtype=I32) % n_new)

    np_, pg = cache_k_a.shape[1], cache_k_a.shape[2]
    caches = (jnp.swapaxes(cache_k_idx, 2, 3),
              cache_k_a.reshape(depth, np_, pg * A_KV, HEAD_DIM),
              cache_v_a.reshape(depth, np_, pg * A_KV, HEAD_DIM),
              cache_k_b.reshape(depth, np_, pg * B_KV * 2, HEAD_DIM),
              cache_v_b.reshape(depth, np_, pg, B_KV, 2, HEAD_DIM).transpose(0, 1, 2, 4, 3, 5)
              .reshape(depth, np_, pg * B_KV * 2, HEAD_DIM))

    hp = _rmsnorm(xp, norm_g[0], BF16)
    hs = _rmsnorm(xs, norm_g[0], BF16)
    rows_p = [[] for _ in range(5)]
    rows_s = [[] for _ in range(5)]
    pad_t = lambda a, axis: jnp.pad(a, [(0, DEC_ROWS - n_new) if ax == axis else (0, 0) for ax in range(a.ndim)])
    for l in range(depth):
        lam_init = _lambda_init(l)
        weights = _split_weights(w_in[l])
        w_pa, w_pb, w_o = w_proj_a[l].astype(BF16), w_proj_b[l].astype(BF16), w_out[l].astype(BF16)
        last = l + 1 == depth
        g_next = final_norm_g if last else norm_g[l + 1]
        norm_dtype = F32 if last else BF16
        lq = lam_qk[l].astype(F32)
        g_sub = subln_g[l].reshape(1, B_VDIM).astype(F32)

        qrot, (kvb, nkb, nvb, nka, nva), (qe, kw, kk), zg = _project_all(hp, weights, tabs_p, nb, lp, n_tok, True)
        o_a = _dsa_prompt(qrot, qe, kw, kk, kvb, nb, lp, k_top_p)
        o_b = _diff_prompt(qrot, kvb, lq, g_sub, nb, lp, lam_init)
        xp, hp = _merge(xp, o_a, o_b, zg, w_pa, w_pb, w_o, g_next, norm_dtype)
        ki_p = kw.reshape(nb, lp, LANES)[:, :n_tok, :IDX_DIM]
        for i, a in enumerate((nka.reshape(nb, n_tok, A_KV, HEAD_DIM), nva.reshape(nb, n_tok, A_KV, HEAD_DIM), ki_p,
                               nkb.reshape(nb, n_tok, B_KV, 2, HEAD_DIM), nvb.reshape(nb, n_tok, B_KV, B_VDIM))):
            rows_p[i].append(a)

        ms = nseq * n_new
        qrot, (_, nkb, nvb, nka, nva), (qi, kw, _), zg = _project_all(hs, weights, tabs_s, 1, ms, ms, False)
        qa8 = pad_t(qrot[:, :A_WIDTH].reshape(nseq, n_new, A_KV, A_GROUP, HEAD_DIM).transpose(0, 2, 3, 1, 4), 3)
        qa8 = qa8.reshape(nseq, A_KV, A_GROUP * DEC_ROWS, HEAD_DIM)
        qb8 = qrot[:, A_WIDTH:].reshape(nseq, n_new, B_KV, B_GROUP, 2, HEAD_DIM).transpose(0, 2, 4, 3, 1, 5)
        qb8 = pad_t(qb8, 4).reshape(nseq, B_KV, 2, B_GROUP * DEC_ROWS, HEAD_DIM)
        qi8 = pad_t(qi.reshape(nseq, n_new, IDX_HEADS, IDX_DIM).transpose(0, 2, 1, 3), 2)
        qi8 = qi8.reshape(nseq, IDX_HEADS * DEC_ROWS, IDX_DIM)
        w8 = pad_t(kw[:, IDX_DIM:IDX_DIM + IDX_HEADS].reshape(nseq, n_new, IDX_HEADS), 1)
        ki_s = kw[:, :IDX_DIM].reshape(nseq, n_new, IDX_DIM)
        new_s = (nka.reshape(nseq, n_new, A_KV, HEAD_DIM), nva.reshape(nseq, n_new, A_KV, HEAD_DIM), ki_s,
                 nkb.reshape(nseq, n_new, B_KV, 2, HEAD_DIM), nvb.reshape(nseq, n_new, B_KV, B_VDIM))
        new_rows = [pad_t(new_s[j], 1) for j in (2, 0, 1, 3, 4)]
        new_rows[1:4] = [a.reshape(nseq, -1, HEAD_DIM) for a in new_rows[1:4]]
        new_rows[0] = jnp.pad(jnp.swapaxes(ki_s, 1, 2), ((0, 0), (0, 0), (0, pg - n_new)))
        new_rows[4] = (new_rows[4].reshape(nseq, DEC_ROWS, B_KV, 2, HEAD_DIM).transpose(0, 1, 3, 2, 4)
                       .reshape(nseq, -1, HEAD_DIM))
        o_a8, o_b8 = _decode_attend(page_table, qi8, w8, qa8, qb8, new_rows, lq, g_sub, caches, l, past_len,
                                    n_new, k_top_s, lam_init)
        o_a = o_a8[:, :n_new].reshape(ms, A_WIDTH)
        o_b = o_b8[:, :n_new].reshape(ms, B_WIDTH)
        xs, hs = _merge(xs, o_a, o_b, zg, w_pa, w_pb, w_o, g_next, norm_dtype)
        for i, a in enumerate(new_s):
            rows_s[i].append(a)

    y_prompt = hp.reshape(nb, lp, d)[:, n_meta:n_tok]
    y_sample = hs.reshape(nseq, n_new, d)
    return (y_prompt, y_sample) + tuple(jnp.stack(r) for r in rows_p) + tuple(jnp.stack(r) for r in rows_s)
```

```python
import functools
import math

import jax
import jax.numpy as jnp
import numpy as np
from jax import lax
from jax.experimental import pallas as pl
from jax.experimental.pallas import tpu as pltpu

HEAD_DIM = 128
A_HEADS = 8
A_KV = 2
A_GROUP = A_HEADS // A_KV
A_WIDTH = A_HEADS * HEAD_DIM
IDX_HEADS = 16
IDX_DIM = 64
B_HEADS = 4
B_KV = 2
B_GROUP = B_HEADS // B_KV
B_VDIM = 2 * HEAD_DIM
B_WIDTH = B_HEADS * B_VDIM
TOP_K_MAX = 256
ROPE_THETA = 10000.0
EPS = 1e-6

LANES = 128
ROW_TILE = 256
GATE_ROW_TILE = 512
GATE_COL_TILE = 2048
KEY_CHUNK = 256
ATTN_STEP_WIDTHS = (4, 2, 1)
DSA_Q_TILE = 256
SEARCH_ROWS = 128
DIFF_Q_TILE = 256
MERGE_TILE = 256
DEC_ROWS = 8
DEC_SEARCH_BITS = 3
VMEM_LIMIT = 48 * 1024 * 1024

LOG2_E = 1.4426950408889634
NEG_BIG = -1e30
INT_MIN = -2 ** 31
KEY_NEG_INF = -2139095041
KEY_POS_INF = 2139095040

F32 = jnp.float32
BF16 = jnp.bfloat16
I32 = jnp.int32

_NT = (((1,), (1,)), ((), ()))


def _params(*sem):
    return pltpu.CompilerParams(dimension_semantics=sem, vmem_limit_bytes=VMEM_LIMIT)


def _lambda_init(layer):
    return 0.8 - 0.6 * math.exp(-0.3 * layer)


def _float_key(x):
    b = lax.bitcast_convert_type(x, I32)
    return b ^ ((b >> 31) & jnp.int32(0x7FFFFFFF))


def _rmsnorm_kernel(x_ref, g_ref, o_ref):
    x = x_ref[...]
    ms = jnp.mean(x * x, axis=-1, keepdims=True)
    o_ref[...] = (x * lax.rsqrt(ms + EPS) * g_ref[...]).astype(o_ref.dtype)


def _rmsnorm(x, g, out_dtype):
    m, d = x.shape
    tm = min(ROW_TILE, m)
    return pl.pallas_call(
        _rmsnorm_kernel,
        grid=(m // tm,),
        in_specs=[pl.BlockSpec((tm, d), lambda i: (i, 0)),
                  pl.BlockSpec((1, d), lambda i: (0, 0))],
        out_specs=pl.BlockSpec((tm, d), lambda i: (i, 0)),
        out_shape=jax.ShapeDtypeStruct((m, d), out_dtype),
        compiler_params=_params("arbitrary"),
        name="rmsnorm",
    )(x, g.reshape(1, d).astype(F32))


def _rope128(a, c, s):
    return a * c + pltpu.roll(a, HEAD_DIM // 2, 1) * s


def _rope64(a, c, s_lo, s_hi):
    return a * c + pltpu.roll(a, LANES - IDX_DIM // 2, 1) * s_lo + pltpu.roll(a, IDX_DIM // 2, 1) * s_hi


def _proj_q_kernel(h_ref, w_ref, c_ref, s_ref, o_ref):
    acc = jnp.dot(h_ref[...], w_ref[...], preferred_element_type=F32)
    c, s = c_ref[...], s_ref[...]
    for j in range(acc.shape[1] // LANES):
        sl = slice(j * LANES, (j + 1) * LANES)
        o_ref[:, sl] = _rope128(acc[:, sl], c, s).astype(o_ref.dtype)


def _proj_kv_kernel(h_ref, w_ref, c_ref, s_ref, kv_ref, kb_ref, vb_ref, ka_ref, va_ref):
    acc = jnp.dot(h_ref[0], w_ref[...], preferred_element_type=F32)
    c, s = c_ref[...], s_ref[...]
    outs = ((kb_ref, 0, 4, True), (vb_ref, 4, 4, False), (ka_ref, 8, 2, True), (va_ref, 10, 2, False))
    for ref, start, count, rot in outs:
        for j in range(count):
            a = acc[:, (start + j) * LANES:(start + j + 1) * LANES]
            if rot:
                a = _rope128(a, c, s)
            ref[0, :, j * LANES:(j + 1) * LANES] = a
            kv_ref[0, :, (start + j) * LANES:(start + j + 1) * LANES] = a.astype(kv_ref.dtype)


def _proj_idx_kernel(h_ref, w_ref, c_ref, slo_ref, shi_ref, ck_ref, klo_ref, khi_ref,
                     q_ref, kw_ref, kk_ref, *, expand):
    acc = jnp.dot(h_ref[...], w_ref[...], preferred_element_type=F32)
    c, slo, shi = c_ref[...], slo_ref[...], shi_ref[...]
    lane = lax.broadcasted_iota(I32, (acc.shape[0], LANES), 1)
    low = lane < IDX_DIM
    for j in range(IDX_HEADS // 2):
        a = _rope64(acc[:, j * LANES:(j + 1) * LANES], c, slo, shi)
        if expand:
            q_ref[:, (2 * j) * LANES:(2 * j + 1) * LANES] = jnp.where(low, a, 0.0).astype(q_ref.dtype)
            q_ref[:, (2 * j + 1) * LANES:(2 * j + 2) * LANES] = jnp.where(low, 0.0, a).astype(q_ref.dtype)
        else:
            q_ref[:, j * LANES:(j + 1) * LANES] = a.astype(q_ref.dtype)
    kw = _rope64(acc[:, IDX_HEADS * IDX_DIM:], ck_ref[...], klo_ref[...], khi_ref[...])
    kw_ref[...] = kw
    kk_ref[...] = jnp.where(low, kw, pltpu.roll(kw, IDX_DIM, 1)).astype(kk_ref.dtype)


def _proj_plain_kernel(h_ref, w_ref, o_ref):
    o_ref[...] = jnp.dot(h_ref[...], w_ref[...], preferred_element_type=F32).astype(o_ref.dtype)


def _row_spec(tm, width, ntab=None):
    if ntab is None:
        return pl.BlockSpec((tm, width), lambda j, i: (i, 0))
    return pl.BlockSpec((tm, width), lambda j, i: (i % ntab, 0))


def _proj_q(h, w, tabs):
    m, d = h.shape
    n = w.shape[1]
    tm = min(ROW_TILE, m)
    ntab = tabs[0].shape[0] // tm
    return pl.pallas_call(
        _proj_q_kernel,
        grid=(1, m // tm),
        in_specs=[_row_spec(tm, d), pl.BlockSpec((d, n), lambda j, i: (0, 0)),
                  _row_spec(tm, LANES, ntab), _row_spec(tm, LANES, ntab)],
        out_specs=_row_spec(tm, n),
        out_shape=jax.ShapeDtypeStruct((m, n), BF16),
        compiler_params=_params("arbitrary", "arbitrary"),
        name="proj_q",
    )(h, w, tabs[0], tabs[1])


def _proj_kv(h3, w, tabs, n_keep):
    nb, lp, d = h3.shape
    n = w.shape[1]
    tm = min(DIFF_Q_TILE, lp)
    row = lambda width: pl.BlockSpec((1, tm, width), lambda b, i: (b, i, 0))
    tab = pl.BlockSpec((tm, LANES), lambda b, i: (i, 0))
    f32_out = lambda width: jax.ShapeDtypeStruct((nb, n_keep, width), F32)
    return pl.pallas_call(
        _proj_kv_kernel,
        grid=(nb, lp // tm),
        in_specs=[row(d), pl.BlockSpec((d, n), lambda b, i: (0, 0)), tab, tab],
        out_specs=[row(n), row(4 * LANES), row(4 * LANES), row(2 * LANES), row(2 * LANES)],
        out_shape=[jax.ShapeDtypeStruct((nb, lp, n), BF16),
                   f32_out(4 * LANES), f32_out(4 * LANES), f32_out(2 * LANES), f32_out(2 * LANES)],
        compiler_params=_params("arbitrary", "arbitrary"),
        name="proj_kv",
    )(h3, w, tabs[0], tabs[1])


def _proj_idx(h, w, tabs, expand):
    m, d = h.shape
    n = w.shape[1]
    tm = min(ROW_TILE, m)
    ntab = tabs[0].shape[0] // tm
    qw = IDX_HEADS * (LANES if expand else IDX_DIM)
    return pl.pallas_call(
        functools.partial(_proj_idx_kernel, expand=expand),
        grid=(1, m // tm),
        in_specs=[_row_spec(tm, d), pl.BlockSpec((d, n), lambda j, i: (0, 0))]
                 + [_row_spec(tm, LANES, ntab)] * 6,
        out_specs=[_row_spec(tm, qw), _row_spec(tm, LANES), _row_spec(tm, LANES)],
        out_shape=[jax.ShapeDtypeStruct((m, qw), BF16),
                   jax.ShapeDtypeStruct((m, LANES), F32),
                   jax.ShapeDtypeStruct((m, LANES), BF16)],
        compiler_params=_params("arbitrary", "arbitrary"),
        name="proj_idx",
    )(h, w, *tabs)


def _proj_plain(h, w, tn):
    m, d = h.shape
    n = w.shape[1]
    tm = GATE_ROW_TILE if m % GATE_ROW_TILE == 0 else min(ROW_TILE, m)
    return pl.pallas_call(
        _proj_plain_kernel,
        grid=(n // tn, m // tm),
        in_specs=[_row_spec(tm, d), pl.BlockSpec((d, tn), lambda j, i: (0, j))],
        out_specs=pl.BlockSpec((tm, tn), lambda j, i: (i, j)),
        out_shape=jax.ShapeDtypeStruct((m, n), F32),
        compiler_params=_params("arbitrary", "arbitrary"),
        name="proj_gate",
    )(h, w)


def _search_threshold(count_ge, rows, k_top, bits=1, unroll=False):
    zero = jnp.zeros((rows, 1), I32)
    t = jnp.where(count_ge(zero) >= k_top, zero, jnp.int32(INT_MIN))

    def refine(shift, nbits, t):
        group = jnp.zeros((rows, 1), I32)
        for j in range(1, 2 ** nbits):
            cand = t | jnp.left_shift(jnp.int32(j), shift)
            group = group + jnp.where(count_ge(cand) >= k_top, 1, 0)
        return t | jnp.left_shift(group, shift)

    n_groups, rest = divmod(31, bits)
    if unroll:
        for it in range(n_groups):
            t = refine(31 - bits * (it + 1), bits, t)
    else:
        t = lax.fori_loop(0, n_groups, lambda it, t: refine(31 - bits * (it + 1), bits, t), t)
    return refine(0, rest, t) if rest else t


def _strict_upper(n):
    r = lax.broadcasted_iota(I32, (n, n), 0)
    c = lax.broadcasted_iota(I32, (n, n), 1)
    return jnp.where(r < c, 1.0, 0.0).astype(BF16)


def _dsa_kernel(qa_ref, qe_ref, kw_ref, kk_ref, ka_ref, va_ref, o_ref, sk_ref, *, k_top):
    tq = qa_ref.shape[0]
    kc = KEY_CHUNK
    hr = SEARCH_ROWS
    i = pl.program_id(1)
    q0 = i * tq
    nkc = (q0 + tq + kc - 1) // kc
    col_iota = lax.broadcasted_iota(I32, (hr, kc), 1)
    row_iota = lax.broadcasted_iota(I32, (hr, kc), 0)

    thr_slabs = []
    for r0 in range(0, tq, hr):
        w_idx = kw_ref[r0:r0 + hr, IDX_DIM:IDX_DIM + IDX_HEADS] * (IDX_DIM ** -0.5)
        w_cols = [jnp.broadcast_to(w_idx[:, h:h + 1], (hr, LANES)) for h in range(IDX_HEADS)]
        q_stack = jnp.concatenate([qe_ref[r0:r0 + hr, h * LANES:(h + 1) * LANES] for h in range(IDX_HEADS)],
                                  axis=0)

        def score_chunk(c, q_stack=q_stack, w_cols=w_cols):
            k0 = pl.multiple_of(c * kc, kc)
            d = lax.dot_general(q_stack, kk_ref[pl.ds(k0, kc), :], _NT, preferred_element_type=F32)
            parts = []
            for j in range(kc // LANES):
                acc = jnp.zeros((hr, LANES), F32)
                for h in range(IDX_HEADS):
                    acc = acc + jnp.maximum(d[h * hr:(h + 1) * hr, j * LANES:(j + 1) * LANES], 0.0) * w_cols[h]
                parts.append(acc)
            return _float_key(jnp.concatenate(parts, axis=1)), k0

        def p1_body(c, carry, r0=r0, score_chunk=score_chunk):
            key, _ = score_chunk(c)
            sk_ref[c, r0:r0 + hr, :] = key
            return carry
        lax.fori_loop(0, nkc - 1, p1_body, 0)
        key, k0 = score_chunk(nkc - 1)
        sk_ref[nkc - 1, r0:r0 + hr, :] = jnp.where(k0 + col_iota <= q0 + r0 + row_iota, key, jnp.int32(INT_MIN))

        def count_cmp(cand, strict, r0=r0):
            cand_b = jnp.broadcast_to(cand, (hr, LANES))
            def body(c, acc):
                for j in range(kc // LANES):
                    kj = sk_ref[c, r0:r0 + hr, j * LANES:(j + 1) * LANES]
                    hit = (kj > cand_b) if strict else (kj >= cand_b)
                    acc = acc + jnp.where(hit, 1.0, 0.0)
                return acc
            acc = lax.fori_loop(0, nkc, body, jnp.zeros((hr, LANES), F32))
            return jnp.sum(acc, axis=1, keepdims=True)

        thr = _search_threshold(lambda cand: count_cmp(cand, False), hr, float(k_top))
        c_ge = count_cmp(thr, False)
        c_gt = count_cmp(thr, True)
        need = float(k_top) - c_gt
        tie_rows = jnp.where((c_ge > float(k_top)) & (thr > jnp.int32(INT_MIN)), 1.0, 0.0)

        @pl.when(jnp.max(tie_rows) > 0.0)
        def _(r0=r0, thr=thr, need=need):
            tri = _strict_upper(kc)
            def body(c, run):
                k = sk_ref[c, r0:r0 + hr, :]
                eq = k == thr
                eqf = jnp.where(eq, 1.0, 0.0)
                before = jnp.dot(eqf.astype(BF16), tri, preferred_element_type=F32) + run
                sk_ref[c, r0:r0 + hr, :] = jnp.where(eq & (before >= need), jnp.int32(INT_MIN), k)
                return run + jnp.sum(eqf, axis=1, keepdims=True)
            lax.fori_loop(0, nkc, body, jnp.zeros((hr, 1), F32))

        thr_slabs.append(thr)
    thr = jnp.concatenate(thr_slabs, axis=0)

    scale2 = HEAD_DIM ** -0.5 * LOG2_E

    def attend(c0, width, carry):
        k0 = pl.multiple_of(c0 * kc, kc)
        key = jnp.concatenate([sk_ref[c0 + u] for u in range(width)], axis=1)
        sel = (key >= thr) & (key > jnp.int32(KEY_NEG_INF)) & (key < jnp.int32(KEY_POS_INF))
        bias = jnp.where(sel, 0.0, NEG_BIG)
        out = []
        for g in range(A_KV):
            qg = jnp.concatenate([qa_ref[:, (g * A_GROUP + r) * HEAD_DIM:(g * A_GROUP + r + 1) * HEAD_DIM]
                                  for r in range(A_GROUP)], axis=0)
            kch = ka_ref[0, pl.ds(k0, width * kc), g * HEAD_DIM:(g + 1) * HEAD_DIM]
            vch = va_ref[0, pl.ds(k0, width * kc), g * HEAD_DIM:(g + 1) * HEAD_DIM]
            s = lax.dot_general(qg, kch, _NT, preferred_element_type=F32)
            ps, stats = [], []
            for r in range(A_GROUP):
                m, l, _ = carry[g * A_GROUP + r]
                sr = s[r * tq:(r + 1) * tq] * scale2 + bias
                m_new = jnp.maximum(m, jnp.max(sr, axis=1, keepdims=True))
                p = jnp.exp2(sr - m_new)
                alpha = jnp.exp2(m - m_new)
                stats.append((m_new, alpha * l + jnp.sum(p, axis=1, keepdims=True), alpha))
                ps.append(p.astype(BF16))
            pv = jnp.dot(jnp.concatenate(ps, axis=0), vch, preferred_element_type=F32)
            for r in range(A_GROUP):
                m_new, l_new, alpha = stats[r]
                out.append((m_new, l_new, alpha * carry[g * A_GROUP + r][2] + pv[r * tq:(r + 1) * tq]))
        return tuple(out)

    init = tuple((jnp.full((tq, 1), NEG_BIG, F32), jnp.zeros((tq, 1), F32),
                  jnp.zeros((tq, HEAD_DIM), F32)) for _ in range(A_HEADS))
    fin, done = init, 0
    for width in ATTN_STEP_WIDTHS:
        n_steps = (nkc - done) // width
        fin = lax.fori_loop(0, n_steps, lambda j, carry, done=done, width=width: attend(done + width * j, width, carry),
                            fin)
        done = done + n_steps * width
    for hh in range(A_HEADS):
        o_ref[:, hh * HEAD_DIM:(hh + 1) * HEAD_DIM] = fin[hh][2] / fin[hh][1]


def _dsa_prompt(qrot, qe, kw, kk, kvb, nb, lp, k_top):
    m = qrot.shape[0]
    tq = DSA_Q_TILE
    nq = lp // tq
    qrow = lambda width: pl.BlockSpec((tq, width), lambda b, i: (b * nq + i, 0))
    return pl.pallas_call(
        functools.partial(_dsa_kernel, k_top=k_top),
        grid=(nb, nq),
        in_specs=[qrow(A_WIDTH), qrow(IDX_HEADS * LANES), qrow(LANES),
                  pl.BlockSpec((lp, LANES), lambda b, i: (b, 0)),
                  pl.BlockSpec((1, lp, A_KV * HEAD_DIM), lambda b, i: (b, 0, 4)),
                  pl.BlockSpec((1, lp, A_KV * HEAD_DIM), lambda b, i: (b, 0, 5))],
        out_specs=qrow(A_WIDTH),
        out_shape=jax.ShapeDtypeStruct((m, A_WIDTH), F32),
        scratch_shapes=[pltpu.VMEM((lp // KEY_CHUNK, tq, KEY_CHUNK), I32)],
        compiler_params=_params("arbitrary", "arbitrary"),
        name="dsa_prompt",
    )(qrot, qe, kw, kk, kvb, kvb)


def _lambda_full(lq_ref, lam_init):
    lq = lq_ref[...]
    a = jnp.sum(lq[0:1] * lq[1:2], axis=1, keepdims=True)
    b = jnp.sum(lq[2:3] * lq[3:4], axis=1, keepdims=True)
    return jnp.exp(a) - jnp.exp(b) + lam_init


def _subln(o, g, lam_init):
    ms = jnp.mean(o * o, axis=-1, keepdims=True)
    return (o * lax.rsqrt(ms + EPS) * g) * (1.0 - lam_init)


def _diff_kernel(qb_ref, kb_ref, vb_ref, lq_ref, g_ref, o_ref, *, lam_init):
    tq = qb_ref.shape[0]
    kc = KEY_CHUNK
    i = pl.program_id(1)
    q0 = i * tq
    n_full = q0 // kc
    n_all = (q0 + tq + kc - 1) // kc
    lam = _lambda_full(lq_ref, lam_init)
    scale2 = HEAD_DIM ** -0.5 * LOG2_E
    rows = B_GROUP * tq
    row_pos = q0 + lax.broadcasted_iota(I32, (tq, kc), 0)
    col_iota = lax.broadcasted_iota(I32, (tq, kc), 1)
    row_pos = jnp.concatenate([row_pos] * B_GROUP, axis=0)
    col_iota = jnp.concatenate([col_iota] * B_GROUP, axis=0)

    combos = [(g, c2) for g in range(B_KV) for c2 in range(2)]

    def step(c0, carry, width, masked):
        k0 = pl.multiple_of(c0 * kc, kc)
        out = []
        for idx, (g, c2) in enumerate(combos):
            m, l, acc = carry[idx]
            qgc = jnp.concatenate(
                [qb_ref[:, ((g * B_GROUP + r) * 2 + c2) * HEAD_DIM:((g * B_GROUP + r) * 2 + c2 + 1) * HEAD_DIM]
                 for r in range(B_GROUP)], axis=0)
            kcol = (g * 2 + c2) * HEAD_DIM
            kch = kb_ref[0, pl.ds(k0, width * kc), kcol:kcol + HEAD_DIM]
            vch = vb_ref[0, pl.ds(k0, width * kc), g * B_VDIM:(g + 1) * B_VDIM]
            s = lax.dot_general(qgc, kch, _NT, preferred_element_type=F32) * scale2
            if masked:
                s = jnp.where(k0 + col_iota <= row_pos, s, NEG_BIG)
            m_new = jnp.maximum(m, jnp.max(s, axis=1, keepdims=True))
            p = jnp.exp2(s - m_new)
            alpha = jnp.exp2(m - m_new)
            l = alpha * l + jnp.sum(p, axis=1, keepdims=True)
            acc = alpha * acc + jnp.dot(p.astype(BF16), vch, preferred_element_type=F32)
            out.append((m_new, l, acc))
        return tuple(out)

    init = tuple((jnp.full((rows, 1), NEG_BIG, F32), jnp.zeros((rows, 1), F32),
                  jnp.zeros((rows, B_VDIM), F32)) for _ in combos)
    carry, done = init, 0
    for width in ATTN_STEP_WIDTHS:
        n_steps = (n_full - done) // width
        carry = lax.fori_loop(0, n_steps, lambda j, cr, done=done, width=width: step(done + width * j, cr, width, False),
                              carry)
        done = done + n_steps * width
    carry = lax.fori_loop(n_full, n_all, lambda c, cr: step(c, cr, 1, True), carry)
    for g in range(B_KV):
        o0, o1 = (carry[g * 2 + c2][2] / carry[g * 2 + c2][1] for c2 in range(2))
        o = _subln(o0 - lam * o1, g_ref[...], lam_init)
        for r in range(B_GROUP):
            hh = g * B_GROUP + r
            o_ref[:, hh * B_VDIM:(hh + 1) * B_VDIM] = o[r * tq:(r + 1) * tq]


def _diff_prompt(qrot, kvb, lam_qk, g_sub, nb, lp, lam_init):
    m = qrot.shape[0]
    tq = DIFF_Q_TILE
    nq = lp // tq
    return pl.pallas_call(
        functools.partial(_diff_kernel, lam_init=lam_init),
        grid=(nb, nq),
        in_specs=[pl.BlockSpec((tq, B_HEADS * 2 * HEAD_DIM), lambda b, i: (b * nq + i, 1)),
                  pl.BlockSpec((1, lp, B_KV * 2 * HEAD_DIM), lambda b, i: (b, 0, 0)),
                  pl.BlockSpec((1, lp, B_KV * B_VDIM), lambda b, i: (b, 0, 1)),
                  pl.BlockSpec((4, HEAD_DIM), lambda b, i: (0, 0)),
                  pl.BlockSpec((1, B_VDIM), lambda b, i: (0, 0))],
        out_specs=pl.BlockSpec((tq, B_WIDTH), lambda b, i: (b * nq + i, 0)),
        out_shape=jax.ShapeDtypeStruct((m, B_WIDTH), F32),
        compiler_params=_params("arbitrary", "arbitrary"),
        name="diff_prompt",
    )(qrot, kvb, kvb, lam_qk, g_sub)


def _sigmoid(x):
    return 1.0 / (1.0 + jnp.exp(-x))


def _merge_kernel(x_ref, oa_ref, ob_ref, za_ref, zb_ref, ga_ref, gb_ref, wa_ref, wb_ref, wo_ref, g_ref,
                  x_out_ref, n_out_ref):
    za, zb = za_ref[...], zb_ref[...]
    ya = (oa_ref[...] * (za * _sigmoid(za))).astype(BF16)
    yb = (ob_ref[...] * (zb * _sigmoid(zb))).astype(BF16)
    branch_a = jnp.dot(ya, wa_ref[...], preferred_element_type=F32)
    branch_b = jnp.dot(yb, wb_ref[...], preferred_element_type=F32)
    mix = _sigmoid(ga_ref[...]) * branch_a + _sigmoid(gb_ref[...]) * branch_b
    x_new = x_ref[...] + jnp.dot(mix.astype(BF16), wo_ref[...], preferred_element_type=F32)
    x_out_ref[...] = x_new
    ms = jnp.mean(x_new * x_new, axis=-1, keepdims=True)
    n_out_ref[...] = (x_new * lax.rsqrt(ms + EPS) * g_ref[...]).astype(n_out_ref.dtype)


def _merge(x, o_a, o_b, zg, w_pa, w_pb, w_o, g_next, norm_dtype):
    m, d = x.shape
    tm = min(MERGE_TILE, m)
    row = lambda width, col: pl.BlockSpec((tm, width), lambda i: (i, col))
    full = lambda a: pl.BlockSpec(a.shape, lambda i: (0, 0), pipeline_mode=pl.Buffered(1))
    return pl.pallas_call(
        _merge_kernel,
        grid=(m // tm,),
        in_specs=[row(d, 0), row(A_WIDTH, 0), row(B_WIDTH, 0),
                  row(A_WIDTH, 0), row(B_WIDTH, 1), row(d, 1), row(d, 2),
                  full(w_pa), full(w_pb), full(w_o), pl.BlockSpec((1, d), lambda i: (0, 0))],
        out_specs=[row(d, 0), row(d, 0)],
        out_shape=[jax.ShapeDtypeStruct((m, d), F32), jax.ShapeDtypeStruct((m, d), norm_dtype)],
        compiler_params=_params("arbitrary"),
        name="merge",
    )(x, o_a, o_b, zg, zg, zg, zg, w_pa, w_pb, w_o, g_next.reshape(1, d).astype(F32))


def _decode_kernel(pt_ref, qi_ref, w_ref, qa_ref, qb_ref, kin_ref, kan_ref, van_ref, kbn_ref, vbn_ref,
                   lq_ref, g_ref, cki_ref, cka_ref, cva_ref, ckb_ref, cvb_ref,
                   oa_ref, ob_ref, bki, bka, bva, bkb, bvb, sems, *, layer, past_len, n_new, k_top, lam_init):
    b = pl.program_id(0)
    nb = pl.num_programs(0)
    n_pages = bki.shape[1] - 1
    page = bki.shape[3]
    s_all = (n_pages + 1) * page
    pairs = ((cki_ref, bki), (cka_ref, bka), (cva_ref, bva), (ckb_ref, bkb), (cvb_ref, bvb))

    def page_copy(src, dst, which, seq, slot, p):
        return pltpu.make_async_copy(src.at[layer, pt_ref[seq, p]], dst.at[slot, p], sems.at[slot, which])

    def start_all(seq, slot):
        for p in range(n_pages):
            for which, (src, dst) in enumerate(pairs):
                page_copy(src, dst, which, seq, slot, p).start(priority=p % 2)

    @pl.when(b == 0)
    def _():
        start_all(0, 0)

    slot = lax.rem(b, 2)

    @pl.when(b + 1 < nb)
    def _():
        start_all(b + 1, 1 - slot)

    for buf, new in ((bki, kin_ref), (bka, kan_ref), (bva, van_ref), (bkb, kbn_ref), (bvb, vbn_ref)):
        buf[slot, n_pages] = jnp.zeros(buf.shape[2:], F32)
        buf[slot, n_pages, 0:new.shape[1]] = new[0]

    for p in range(n_pages):
        for which, (src, dst) in enumerate(pairs):
            page_copy(src, dst, which, b, slot, p).wait()

    rows = DEC_ROWS
    col = lax.broadcasted_iota(I32, (rows, s_all), 1)
    tok = jnp.minimum(lax.broadcasted_iota(I32, (rows, s_all), 0), n_new - 1)
    causal = (col <= past_len + tok) & (col < past_len + n_new)

    kidx_t = jnp.concatenate([bki[slot, p] for p in range(n_pages + 1)], axis=1).astype(BF16)
    d = jnp.dot(qi_ref[0], kidx_t, preferred_element_type=F32)
    w_idx = w_ref[0] * (IDX_DIM ** -0.5)
    acc = jnp.zeros((rows, s_all), F32)
    for h in range(IDX_HEADS):
        acc = acc + jnp.maximum(d[h * rows:(h + 1) * rows], 0.0) * w_idx[:, h:h + 1]
    key = jnp.where(causal, _float_key(acc), jnp.int32(INT_MIN))

    def count_ge(cand):
        return jnp.sum(jnp.where(key >= cand, 1.0, 0.0), axis=1, keepdims=True)

    scale = HEAD_DIM ** -0.5

    def softmax_pv(s, v):
        m = jnp.max(s, axis=1, keepdims=True)
        p = jnp.exp(s - m)
        l = jnp.sum(p, axis=1, keepdims=True)
        return jnp.dot(p.astype(v.dtype), v, preferred_element_type=F32) / l

    lam = _lambda_full(lq_ref, lam_init)
    causal_bias = jnp.where(causal, 0.0, NEG_BIG)
    bias_b = jnp.concatenate([causal_bias] * B_GROUP, axis=0)
    for g in range(B_KV):
        vg = jnp.concatenate(
            [bvb[slot, :, pl.ds(hf * B_KV + g, page, stride=2 * B_KV), :].reshape(s_all, HEAD_DIM)
             for hf in range(2)], axis=1).astype(BF16)
        outs = []
        for c2 in range(2):
            kgc = bkb[slot, :, pl.ds(g * 2 + c2, page, stride=2 * B_KV), :].reshape(s_all, HEAD_DIM).astype(BF16)
            s = lax.dot_general(qb_ref[0, g, c2], kgc, _NT, preferred_element_type=F32) * scale + bias_b
            outs.append(softmax_pv(s, vg))
        o = _subln(outs[0] - lam * outs[1], g_ref[...], lam_init)
        for r in range(B_GROUP):
            hh = g * B_GROUP + r
            ob_ref[0, :, hh * B_VDIM:(hh + 1) * B_VDIM] = o[r * rows:(r + 1) * rows]

    thr = _search_threshold(count_ge, rows, float(k_top), bits=DEC_SEARCH_BITS, unroll=True)
    c_ge = count_ge(thr)
    c_gt = jnp.sum(jnp.where(key > thr, 1.0, 0.0), axis=1, keepdims=True)
    need = float(k_top) - c_gt
    sel = (key >= thr) & (key > jnp.int32(KEY_NEG_INF)) & (key < jnp.int32(KEY_POS_INF))
    sel_bias = jnp.where(sel, 0.0, NEG_BIG)

    def drop_ties(sel_bias):
        eq = key == thr
        tri = _strict_upper(LANES)
        run = jnp.zeros((rows, 1), F32)
        drops = []
        for j in range(s_all // LANES):
            eqj = jnp.where(eq[:, j * LANES:(j + 1) * LANES], 1.0, 0.0)
            before = jnp.dot(eqj.astype(BF16), tri, preferred_element_type=F32) + run
            drops.append(before >= need)
            run = run + jnp.sum(eqj, axis=1, keepdims=True)
        drop = eq & jnp.concatenate(drops, axis=1) & (c_ge > float(k_top))
        return jnp.where(drop, NEG_BIG, sel_bias)

    tie_rows = jnp.where((c_ge > float(k_top)) & (thr > jnp.int32(INT_MIN)), 1.0, 0.0)
    sel_bias = lax.cond(jnp.max(tie_rows) > 0.0, drop_ties, lambda x: x, sel_bias)

    bias_a = jnp.concatenate([sel_bias] * A_GROUP, axis=0)
    for g in range(A_KV):
        kg = bka[slot, :, pl.ds(g, page, stride=A_KV), :].reshape(s_all, HEAD_DIM).astype(BF16)
        vg = bva[slot, :, pl.ds(g, page, stride=A_KV), :].reshape(s_all, HEAD_DIM).astype(BF16)
        s = lax.dot_general(qa_ref[0, g], kg, _NT, preferred_element_type=F32) * scale + bias_a
        o = softmax_pv(s, vg)
        for r in range(A_GROUP):
            hh = g * A_GROUP + r
            oa_ref[0, :, hh * HEAD_DIM:(hh + 1) * HEAD_DIM] = o[r * rows:(r + 1) * rows]


def _decode_attend(page_table, qi8, w8, qa8, qb8, new_rows, lam_qk, g_sub, caches, layer, past_len, n_new,
                   k_top, lam_init):
    nseq, n_pages = page_table.shape
    page = caches[0].shape[2]
    seq_spec = lambda a: pl.BlockSpec((1,) + a.shape[1:], lambda b, pt: (b,) + (0,) * (a.ndim - 1))
    const = lambda a: pl.BlockSpec(a.shape, lambda b, pt: (0,) * a.ndim)
    any_spec = pl.BlockSpec(memory_space=pl.ANY)
    per_seq = [qi8, w8, qa8, qb8] + list(new_rows)
    buf = lambda c: pltpu.VMEM((2, n_pages + 1) + c.shape[2:], F32)
    grid_spec = pltpu.PrefetchScalarGridSpec(
        num_scalar_prefetch=1,
        grid=(nseq,),
        in_specs=[seq_spec(a) for a in per_seq] + [const(lam_qk), const(g_sub)] + [any_spec] * 5,
        out_specs=[pl.BlockSpec((1, DEC_ROWS, A_WIDTH), lambda b, pt: (b, 0, 0)),
                   pl.BlockSpec((1, DEC_ROWS, B_WIDTH), lambda b, pt: (b, 0, 0))],
        scratch_shapes=[buf(c) for c in caches] + [pltpu.SemaphoreType.DMA((2, 5))],
    )
    return pl.pallas_call(
        functools.partial(_decode_kernel, layer=layer, past_len=past_len, n_new=n_new, k_top=k_top,
                          lam_init=lam_init),
        grid_spec=grid_spec,
        out_shape=[jax.ShapeDtypeStruct((nseq, DEC_ROWS, A_WIDTH), F32),
                   jax.ShapeDtypeStruct((nseq, DEC_ROWS, B_WIDTH), F32)],
        compiler_params=_params("arbitrary"),
        name="decode_attend",
    )(page_table, *per_seq, lam_qk, g_sub, *caches)


def _rope_tables(pos):
    def cs(dim):
        half = dim // 2
        inv = ROPE_THETA ** (-jnp.arange(half, dtype=F32) / half)
        ang = pos.astype(F32)[:, None] * inv[None, :]
        return jnp.cos(ang), jnp.sin(ang)
    cos, sin = cs(HEAD_DIM)
    c128 = jnp.concatenate([cos, cos], axis=1)
    s128 = jnp.concatenate([-sin, sin], axis=1)
    cos, sin = cs(IDX_DIM)
    zero = jnp.zeros_like(sin)
    c64 = jnp.concatenate([cos, cos, cos, cos], axis=1)
    lo64 = jnp.concatenate([-sin, zero, -sin, zero], axis=1)
    hi64 = jnp.concatenate([zero, sin, zero, sin], axis=1)
    n = pos.shape[0]
    tail = jnp.concatenate([jnp.full((n, IDX_HEADS), IDX_HEADS ** -0.5, F32),
                            jnp.zeros((n, LANES - IDX_DIM - IDX_HEADS), F32)], axis=1)
    ck = jnp.concatenate([cos, cos, tail], axis=1)
    zero_tail = jnp.zeros((n, LANES - IDX_DIM), F32)
    klo = jnp.concatenate([-sin, zero, zero_tail], axis=1)
    khi = jnp.concatenate([zero, sin, zero_tail], axis=1)
    return (c128, s128), (c64, lo64, hi64, ck, klo, khi)


def _split_weights(w):
    sizes = (A_WIDTH, A_KV * HEAD_DIM, A_KV * HEAD_DIM, A_WIDTH, IDX_HEADS * IDX_DIM, IDX_DIM, IDX_HEADS,
             B_HEADS * 2 * HEAD_DIM, B_KV * 2 * HEAD_DIM, B_KV * B_VDIM, B_WIDTH, w.shape[0], w.shape[0])
    pts = np.cumsum(sizes)[:-1]
    q_a, k_a, v_a, z_a, qi, ki, wi, q_b, k_b, v_b, z_b, g_a, g_b = jnp.split(w, pts, axis=1)
    pad = jnp.zeros((w.shape[0], LANES - IDX_DIM - IDX_HEADS), w.dtype)
    cat = lambda parts: jnp.concatenate(parts, axis=1).astype(BF16)
    return (cat([q_a, q_b]), cat([k_b, v_b, k_a, v_a]), cat([qi, ki, wi, pad]), cat([z_a, z_b, g_a, g_b]))


def _project_all(h, weights, tabs, nb, lp, n_keep, expand):
    w_q, w_kv, w_idx, w_zg = weights
    qrot = _proj_q(h, w_q, tabs[0])
    kv = _proj_kv(h.reshape(nb, lp, h.shape[1]), w_kv, tabs[0], n_keep)
    idx = _proj_idx(h, w_idx, tabs[1], expand)
    zg = _proj_plain(h, w_zg, GATE_COL_TILE)
    return qrot, kv, idx, zg


def kernel(x_prompt, x_sample, cache_k_a, cache_v_a, cache_k_idx, cache_k_b, cache_v_b, page_table,
           meta_tokens, norm_g, w_in, lam_qk, subln_g, w_proj_a, w_proj_b, w_out, final_norm_g):
    nb, seq, d = x_prompt.shape
    nseq, n_new, _ = x_sample.shape
    depth = norm_g.shape[0]
    n_meta = meta_tokens.shape[0]
    n_pages = page_table.shape[1]
    page = cache_k_a.shape[2]
    past_len = n_pages * page
    n_tok = n_meta + seq
    lp = -(-n_tok // ROW_TILE) * ROW_TILE
    k_top_p = min(TOP_K_MAX, seq // 4)
    k_top_s = min(TOP_K_MAX, (past_len + n_new) // 4)

    meta = jnp.broadcast_to(meta_tokens.astype(x_prompt.dtype)[None], (nb, n_meta, d))
    xp = jnp.concatenate([meta, x_prompt, jnp.zeros((nb, lp - n_tok, d), x_prompt.dtype)], axis=1)
    xp = xp.reshape(nb * lp, d)
    xs = x_sample.reshape(nseq * n_new, d)
    tabs_p = _rope_tables(jnp.arange(lp, dtype=I32))
    tabs_s = _rope_tables(past_len + jnp.arange(nseq * n_new, dtype=I32) % n_new)

    np_, pg = cache_k_a.shape[1], cache_k_a.shape[2]
    caches = (jnp.swapaxes(cache_k_idx, 2, 3),
              cache_k_a.reshape(depth, np_, pg * A_KV, HEAD_DIM),
              cache_v_a.reshape(depth, np_, pg * A_KV, HEAD_DIM),
              cache_k_b.reshape(depth, np_, pg * B_KV * 2, HEAD_DIM),
              cache_v_b.reshape(depth, np_, pg, B_KV, 2, HEAD_DIM).transpose(0, 1, 2, 4, 3, 5)
              .reshape(depth, np_, pg * B_KV * 2, HEAD_DIM))

    hp = _rmsnorm(xp, norm_g[0], BF16)
    hs = _rmsnorm(xs, norm_g[0], BF16)
    rows_p = [[] for _ in range(5)]
    rows_s = [[] for _ in range(5)]
    pad_t = lambda a, axis: jnp.pad(a, [(0, DEC_ROWS - n_new) if ax == axis else (0, 0) for ax in range(a.ndim)])
    for l in range(depth):
        lam_init = _lambda_init(l)
        weights = _split_weights(w_in[l])
        w_pa, w_pb, w_o = w_proj_a[l].astype(BF16), w_proj_b[l].astype(BF16), w_out[l].astype(BF16)
        last = l + 1 == depth
        g_next = final_norm_g if last else norm_g[l + 1]
        norm_dtype = F32 if last else BF16
        lq = lam_qk[l].astype(F32)
        g_sub = subln_g[l].reshape(1, B_VDIM).astype(F32)

        qrot, (kvb, nkb, nvb, nka, nva), (qe, kw, kk), zg = _project_all(hp, weights, tabs_p, nb, lp, n_tok, True)
        o_a = _dsa_prompt(qrot, qe, kw, kk, kvb, nb, lp, k_top_p)
        o_b = _diff_prompt(qrot, kvb, lq, g_sub, nb, lp, lam_init)
        xp, hp = _merge(xp, o_a, o_b, zg, w_pa, w_pb, w_o, g_next, norm_dtype)
        ki_p = kw.reshape(nb, lp, LANES)[:, :n_tok, :IDX_DIM]
        for i, a in enumerate((nka.reshape(nb, n_tok, A_KV, HEAD_DIM), nva.reshape(nb, n_tok, A_KV, HEAD_DIM), ki_p,
                               nkb.reshape(nb, n_tok, B_KV, 2, HEAD_DIM), nvb.reshape(nb, n_tok, B_KV, B_VDIM))):
            rows_p[i].append(a)

        ms = nseq * n_new
        qrot, (_, nkb, nvb, nka, nva), (qi, kw, _), zg = _project_all(hs, weights, tabs_s, 1, ms, ms, False)
        qa8 = pad_t(qrot[:, :A_WIDTH].reshape(nseq, n_new, A_KV, A_GROUP, HEAD_DIM).transpose(0, 2, 3, 1, 4), 3)
        qa8 = qa8.reshape(nseq, A_KV, A_GROUP * DEC_ROWS, HEAD_DIM)
        qb8 = qrot[:, A_WIDTH:].reshape(nseq, n_new, B_KV, B_GROUP, 2, HEAD_DIM).transpose(0, 2, 4, 3, 1, 5)
        qb8 = pad_t(qb8, 4).reshape(nseq, B_KV, 2, B_GROUP * DEC_ROWS, HEAD_DIM)
        qi8 = pad_t(qi.reshape(nseq, n_new, IDX_HEADS, IDX_DIM).transpose(0, 2, 1, 3), 2)
        qi8 = qi8.reshape(nseq, IDX_HEADS * DEC_ROWS, IDX_DIM)
        w8 = pad_t(kw[:, IDX_DIM:IDX_DIM + IDX_HEADS].reshape(nseq, n_new, IDX_HEADS), 1)
        ki_s = kw[:, :IDX_DIM].reshape(nseq, n_new, IDX_DIM)
        new_s = (nka.reshape(nseq, n_new, A_KV, HEAD_DIM), nva.reshape(nseq, n_new, A_KV, HEAD_DIM), ki_s,
                 nkb.reshape(nseq, n_new, B_KV, 2, HEAD_DIM), nvb.reshape(nseq, n_new, B_KV, B_VDIM))
        new_rows = [pad_t(new_s[j], 1) for j in (2, 0, 1, 3, 4)]
        new_rows[1:4] = [a.reshape(nseq, -1, HEAD_DIM) for a in new_rows[1:4]]
        new_rows[0] = jnp.pad(jnp.swapaxes(ki_s, 1, 2), ((0, 0), (0, 0), (0, pg - n_new)))
        new_rows[4] = (new_rows[4].reshape(nseq, DEC_ROWS, B_KV, 2, HEAD_DIM).transpose(0, 1, 3, 2, 4)
                       .reshape(nseq, -1, HEAD_DIM))
        o_a8, o_b8 = _decode_attend(page_table, qi8, w8, qa8, qb8, new_rows, lq, g_sub, caches, l, past_len,
                                    n_new, k_top_s, lam_init)
        o_a = o_a8[:, :n_new].reshape(ms, A_WIDTH)
        o_b = o_b8[:, :n_new].reshape(ms, B_WIDTH)
        xs, hs = _merge(xs, o_a, o_b, zg, w_pa, w_pb, w_o, g_next, norm_dtype)
        for i, a in enumerate(new_s):
            rows_s[i].append(a)

    y_prompt = hp.reshape(nb, lp, d)[:, n_meta:n_tok]
    y_sample = hs.reshape(nseq, n_new, d)
    return (y_prompt, y_sample) + tuple(jnp.stack(r) for r in rows_p) + tuple(jnp.stack(r) for r in rows_s)
```

```python
import functools
import math

import jax
import jax.numpy as jnp
import numpy as np
from jax import lax
from jax.experimental import pallas as pl
from jax.experimental.pallas import tpu as pltpu

HEAD_DIM = 128
A_HEADS = 8
A_KV = 2
A_GROUP = A_HEADS // A_KV
A_WIDTH = A_HEADS * HEAD_DIM
IDX_HEADS = 16
IDX_DIM = 64
B_HEADS = 4
B_KV = 2
B_GROUP = B_HEADS // B_KV
B_VDIM = 2 * HEAD_DIM
B_WIDTH = B_HEADS * B_VDIM
TOP_K_MAX = 256
ROPE_THETA = 10000.0
EPS = 1e-6

LANES = 128
ROW_TILE = 256
GATE_ROW_TILE = 512
GATE_COL_TILE = 2048
KEY_CHUNK = 256
ATTN_STEP_WIDTHS = (4, 2, 1)
DSA_Q_TILE = 256
SEARCH_ROWS = 128
SCORE_ROWS = 64
DIFF_Q_TILE = 256
MERGE_TILE = 256
DEC_ROWS = 8
DEC_SEARCH_BITS = 3
VMEM_LIMIT = 48 * 1024 * 1024

LOG2_E = 1.4426950408889634
NEG_BIG = -1e30
INT_MIN = -2 ** 31
KEY_NEG_INF = -2139095041

F32 = jnp.float32
BF16 = jnp.bfloat16
I32 = jnp.int32

_NT = (((1,), (1,)), ((), ()))


def _params(*sem):
    return pltpu.CompilerParams(dimension_semantics=sem, vmem_limit_bytes=VMEM_LIMIT)


def _lambda_init(layer):
    return 0.8 - 0.6 * math.exp(-0.3 * layer)


def _rmsnorm_kernel(x_ref, g_ref, o_ref):
    x = x_ref[...]
    ms = jnp.mean(x * x, axis=-1, keepdims=True)
    o_ref[...] = (x * lax.rsqrt(ms + EPS) * g_ref[...]).astype(o_ref.dtype)


def _rmsnorm(x, g, out_dtype):
    m, d = x.shape
    tm = min(ROW_TILE, m)
    return pl.pallas_call(
        _rmsnorm_kernel,
        grid=(m // tm,),
        in_specs=[pl.BlockSpec((tm, d), lambda i: (i, 0)),
                  pl.BlockSpec((1, d), lambda i: (0, 0))],
        out_specs=pl.BlockSpec((tm, d), lambda i: (i, 0)),
        out_shape=jax.ShapeDtypeStruct((m, d), out_dtype),
        compiler_params=_params("arbitrary"),
        name="rmsnorm",
    )(x, g.reshape(1, d).astype(F32))


def _rope128(a, c, s):
    return a * c + pltpu.roll(a, HEAD_DIM // 2, 1) * s


def _rope64(a, c, s_lo, s_hi):
    return a * c + pltpu.roll(a, LANES - IDX_DIM // 2, 1) * s_lo + pltpu.roll(a, IDX_DIM // 2, 1) * s_hi


def _proj_q_kernel(h_ref, w_ref, c_ref, s_ref, o_ref):
    acc = jnp.dot(h_ref[...], w_ref[...], preferred_element_type=F32)
    c, s = c_ref[...], s_ref[...]
    for j in range(acc.shape[1] // LANES):
        sl = slice(j * LANES, (j + 1) * LANES)
        o_ref[:, sl] = _rope128(acc[:, sl], c, s).astype(o_ref.dtype)


def _proj_kv_kernel(h_ref, w_ref, c_ref, s_ref, kv_ref, kb_ref, vb_ref, ka_ref, va_ref):
    acc = jnp.dot(h_ref[0], w_ref[...], preferred_element_type=F32)
    c, s = c_ref[...], s_ref[...]
    outs = ((kb_ref, 0, 4, True), (vb_ref, 4, 4, False), (ka_ref, 8, 2, True), (va_ref, 10, 2, False))
    for ref, start, count, rot in outs:
        for j in range(count):
            a = acc[:, (start + j) * LANES:(start + j + 1) * LANES]
            if rot:
                a = _rope128(a, c, s)
            ref[0, :, j * LANES:(j + 1) * LANES] = a
            kv_ref[0, :, (start + j) * LANES:(start + j + 1) * LANES] = a.astype(kv_ref.dtype)


def _proj_idx_kernel(h_ref, w_ref, c_ref, slo_ref, shi_ref, ck_ref, klo_ref, khi_ref,
                     q_ref, kw_ref, kk_ref, *, expand):
    acc = jnp.dot(h_ref[...], w_ref[...], preferred_element_type=F32)
    c, slo, shi = c_ref[...], slo_ref[...], shi_ref[...]
    lane = lax.broadcasted_iota(I32, (acc.shape[0], LANES), 1)
    low = lane < IDX_DIM
    for j in range(IDX_HEADS // 2):
        a = _rope64(acc[:, j * LANES:(j + 1) * LANES], c, slo, shi)
        if expand:
            q_ref[:, (2 * j) * LANES:(2 * j + 1) * LANES] = jnp.where(low, a, 0.0).astype(q_ref.dtype)
            q_ref[:, (2 * j + 1) * LANES:(2 * j + 2) * LANES] = jnp.where(low, 0.0, a).astype(q_ref.dtype)
        else:
            q_ref[:, j * LANES:(j + 1) * LANES] = a.astype(q_ref.dtype)
    kw = _rope64(acc[:, IDX_HEADS * IDX_DIM:], ck_ref[...], klo_ref[...], khi_ref[...])
    kw_ref[...] = kw
    kk_ref[...] = jnp.where(low, kw, pltpu.roll(kw, IDX_DIM, 1)).astype(kk_ref.dtype)


def _proj_plain_kernel(h_ref, w_ref, o_ref):
    o_ref[...] = jnp.dot(h_ref[...], w_ref[...], preferred_element_type=F32).astype(o_ref.dtype)


def _row_spec(tm, width, ntab=None):
    if ntab is None:
        return pl.BlockSpec((tm, width), lambda j, i: (i, 0))
    return pl.BlockSpec((tm, width), lambda j, i: (i % ntab, 0))


def _proj_q(h, w, tabs):
    m, d = h.shape
    n = w.shape[1]
    tm = min(ROW_TILE, m)
    ntab = tabs[0].shape[0] // tm
    return pl.pallas_call(
        _proj_q_kernel,
        grid=(1, m // tm),
        in_specs=[_row_spec(tm, d), pl.BlockSpec((d, n), lambda j, i: (0, 0)),
                  _row_spec(tm, LANES, ntab), _row_spec(tm, LANES, ntab)],
        out_specs=_row_spec(tm, n),
        out_shape=jax.ShapeDtypeStruct((m, n), BF16),
        compiler_params=_params("arbitrary", "arbitrary"),
        name="proj_q",
    )(h, w, tabs[0], tabs[1])


def _proj_kv(h3, w, tabs, n_keep):
    nb, lp, d = h3.shape
    n = w.shape[1]
    tm = min(DIFF_Q_TILE, lp)
    row = lambda width: pl.BlockSpec((1, tm, width), lambda b, i: (b, i, 0))
    tab = pl.BlockSpec((tm, LANES), lambda b, i: (i, 0))
    f32_out = lambda width: jax.ShapeDtypeStruct((nb, n_keep, width), F32)
    return pl.pallas_call(
        _proj_kv_kernel,
        grid=(nb, lp // tm),
        in_specs=[row(d), pl.BlockSpec((d, n), lambda b, i: (0, 0)), tab, tab],
        out_specs=[row(n), row(4 * LANES), row(4 * LANES), row(2 * LANES), row(2 * LANES)],
        out_shape=[jax.ShapeDtypeStruct((nb, lp, n), BF16),
                   f32_out(4 * LANES), f32_out(4 * LANES), f32_out(2 * LANES), f32_out(2 * LANES)],
        compiler_params=_params("arbitrary", "arbitrary"),
        name="proj_kv",
    )(h3, w, tabs[0], tabs[1])


def _proj_idx(h, w, tabs, expand):
    m, d = h.shape
    n = w.shape[1]
    tm = min(ROW_TILE, m)
    ntab = tabs[0].shape[0] // tm
    qw = IDX_HEADS * (LANES if expand else IDX_DIM)
    return pl.pallas_call(
        functools.partial(_proj_idx_kernel, expand=expand),
        grid=(1, m // tm),
        in_specs=[_row_spec(tm, d), pl.BlockSpec((d, n), lambda j, i: (0, 0))]
                 + [_row_spec(tm, LANES, ntab)] * 6,
        out_specs=[_row_spec(tm, qw), _row_spec(tm, LANES), _row_spec(tm, LANES)],
        out_shape=[jax.ShapeDtypeStruct((m, qw), BF16),
                   jax.ShapeDtypeStruct((m, LANES), F32),
                   jax.ShapeDtypeStruct((m, LANES), BF16)],
        compiler_params=_params("arbitrary", "arbitrary"),
        name="proj_idx",
    )(h, w, *tabs)


def _proj_plain(h, w, tn):
    m, d = h.shape
    n = w.shape[1]
    tm = GATE_ROW_TILE if m % GATE_ROW_TILE == 0 else min(ROW_TILE, m)
    return pl.pallas_call(
        _proj_plain_kernel,
        grid=(n // tn, m // tm),
        in_specs=[_row_spec(tm, d), pl.BlockSpec((d, tn), lambda j, i: (0, j))],
        out_specs=pl.BlockSpec((tm, tn), lambda j, i: (i, j)),
        out_shape=jax.ShapeDtypeStruct((m, n), F32),
        compiler_params=_params("arbitrary", "arbitrary"),
        name="proj_gate",
    )(h, w)


def _key_value(key):
    return lax.bitcast_convert_type(key ^ ((key >> 31) & jnp.int32(0x7FFFFFFF)), F32)


def _one_zero(x):
    return jnp.where(x == 0.0, 0.0, x)


def _threshold_value(thr_key):
    return jnp.where(thr_key < jnp.int32(KEY_NEG_INF), -jnp.inf, _key_value(thr_key))


def _search_threshold(count_ge, rows, k_top, bits=1, unroll=False):
    zero = jnp.zeros((rows, 1), I32)
    t = jnp.where(count_ge(zero) >= k_top, zero, jnp.int32(INT_MIN))
    n_bits = 31

    def refine(shift, nbits, t):
        group = jnp.zeros((rows, 1), I32)
        for j in range(1, 2 ** nbits):
            cand = t | jnp.left_shift(jnp.int32(j), shift)
            group = group + jnp.where(count_ge(cand) >= k_top, 1, 0)
        return t | jnp.left_shift(group, shift)

    n_groups, rest = divmod(n_bits, bits)
    if unroll:
        for it in range(n_groups):
            t = refine(n_bits - bits * (it + 1), bits, t)
    else:
        t = lax.fori_loop(0, n_groups, lambda it, t: refine(n_bits - bits * (it + 1), bits, t), t)
    return refine(0, rest, t) if rest else t


def _strict_upper(n):
    r = lax.broadcasted_iota(I32, (n, n), 0)
    c = lax.broadcasted_iota(I32, (n, n), 1)
    return jnp.where(r < c, 1.0, 0.0).astype(BF16)


def _dsa_kernel(qa_ref, qe_ref, kw_ref, kk_ref, ka_ref, va_ref, o_ref, sc_ref, *, k_top):
    tq = qa_ref.shape[0]
    kc = KEY_CHUNK
    hr = SEARCH_ROWS
    sr = SCORE_ROWS
    i = pl.program_id(1)
    q0 = i * tq
    nkc = (q0 + tq + kc - 1) // kc
    col_iota = lax.broadcasted_iota(I32, (sr, kc), 1)
    row_iota = lax.broadcasted_iota(I32, (sr, kc), 0)

    slabs = list(range(0, tq, sr))

    def head_dots(c, r0):
        q_stack = jnp.concatenate([qe_ref[r0:r0 + sr, h * LANES:(h + 1) * LANES] for h in range(IDX_HEADS)],
                                  axis=0)
        k0 = pl.multiple_of(c * kc, kc)
        return lax.dot_general(q_stack, kk_ref[pl.ds(k0, kc), :], _NT, preferred_element_type=F32)

    w_cols = {}
    for r0 in slabs:
        w_idx = kw_ref[r0:r0 + sr, IDX_DIM:IDX_DIM + IDX_HEADS] * (IDX_DIM ** -0.5)
        w_cols[r0] = [jnp.broadcast_to(w_idx[:, h:h + 1], (sr, LANES)) for h in range(IDX_HEADS)]

    def slab_keys(d, r0):
        parts = []
        for j in range(kc // LANES):
            acc = jnp.zeros((sr, LANES), F32)
            for h in range(IDX_HEADS):
                acc = acc + jnp.maximum(d[h * sr:(h + 1) * sr, j * LANES:(j + 1) * LANES], 0.0) * w_cols[r0][h]
            parts.append(acc)
        return _one_zero(jnp.concatenate(parts, axis=1))

    def p1_body(c, carry):
        for r0 in slabs:
            sc_ref[c, r0:r0 + sr, :] = slab_keys(head_dots(c, r0), r0)
        return carry
    lax.fori_loop(0, nkc - 1, p1_body, 0)
    for r0 in slabs:
        causal = (nkc - 1) * kc + col_iota <= q0 + r0 + row_iota
        sc_ref[nkc - 1, r0:r0 + sr, :] = jnp.where(causal, slab_keys(head_dots(nkc - 1, r0), r0), jnp.nan)

    thr_slabs = []
    for r0 in range(0, tq, SEARCH_ROWS):
        def count_cmp(cand, strict, r0=r0):
            cand_b = jnp.broadcast_to(cand, (hr, LANES))
            def body(c, acc):
                for j in range(kc // LANES):
                    sj = sc_ref[c, r0:r0 + hr, j * LANES:(j + 1) * LANES]
                    hit = (sj > cand_b) if strict else (sj >= cand_b)
                    acc = acc + jnp.where(hit, 1.0, 0.0)
                return acc
            acc = lax.fori_loop(0, nkc, body, jnp.zeros((hr, LANES), F32))
            return jnp.sum(acc, axis=1, keepdims=True)

        thr_key = _search_threshold(lambda key: count_cmp(_key_value(key), False), hr, float(k_top))
        thr = _threshold_value(thr_key)
        c_ge = count_cmp(thr, False)
        c_gt = count_cmp(thr, True)
        need = float(k_top) - c_gt
        tie_rows = jnp.where((c_ge > float(k_top)) & (thr_key > jnp.int32(INT_MIN)), 1.0, 0.0)

        @pl.when(jnp.max(tie_rows) > 0.0)
        def _(r0=r0, thr=thr, need=need):
            tri = _strict_upper(kc)
            def body(c, run):
                s = sc_ref[c, r0:r0 + hr, :]
                eq = s == thr
                eqf = jnp.where(eq, 1.0, 0.0)
                before = jnp.dot(eqf.astype(BF16), tri, preferred_element_type=F32) + run
                sc_ref[c, r0:r0 + hr, :] = jnp.where(eq & (before >= need), jnp.nan, s)
                return run + jnp.sum(eqf, axis=1, keepdims=True)
            lax.fori_loop(0, nkc, body, jnp.zeros((hr, 1), F32))

        thr_slabs.append(thr)
    thr = jnp.concatenate(thr_slabs, axis=0)

    scale2 = HEAD_DIM ** -0.5 * LOG2_E

    def attend(c0, width, carry):
        k0 = pl.multiple_of(c0 * kc, kc)
        score = jnp.concatenate([sc_ref[c0 + u] for u in range(width)], axis=1)
        sel = (score >= thr) & (score > -jnp.inf) & (score < jnp.inf)
        bias = jnp.where(sel, 0.0, NEG_BIG)
        out = []
        for g in range(A_KV):
            qg = jnp.concatenate([qa_ref[:, (g * A_GROUP + r) * HEAD_DIM:(g * A_GROUP + r + 1) * HEAD_DIM]
                                  for r in range(A_GROUP)], axis=0)
            kch = ka_ref[0, pl.ds(k0, width * kc), g * HEAD_DIM:(g + 1) * HEAD_DIM]
            vch = va_ref[0, pl.ds(k0, width * kc), g * HEAD_DIM:(g + 1) * HEAD_DIM]
            s = lax.dot_general(qg, kch, _NT, preferred_element_type=F32)
            ps, stats = [], []
            for r in range(A_GROUP):
                m, l, _ = carry[g * A_GROUP + r]
                sr = s[r * tq:(r + 1) * tq] * scale2 + bias
                m_new = jnp.maximum(m, jnp.max(sr, axis=1, keepdims=True))
                p = jnp.exp2(sr - m_new)
                alpha = jnp.exp2(m - m_new)
                stats.append((m_new, alpha * l + jnp.sum(p, axis=1, keepdims=True), alpha))
                ps.append(p.astype(BF16))
            pv = jnp.dot(jnp.concatenate(ps, axis=0), vch, preferred_element_type=F32)
            for r in range(A_GROUP):
                m_new, l_new, alpha = stats[r]
                out.append((m_new, l_new, alpha * carry[g * A_GROUP + r][2] + pv[r * tq:(r + 1) * tq]))
        return tuple(out)

    init = tuple((jnp.full((tq, 1), NEG_BIG, F32), jnp.zeros((tq, 1), F32),
                  jnp.zeros((tq, HEAD_DIM), F32)) for _ in range(A_HEADS))
    fin, done = init, 0
    for width in ATTN_STEP_WIDTHS:
        n_steps = (nkc - done) // width
        fin = lax.fori_loop(0, n_steps, lambda j, carry, done=done, width=width: attend(done + width * j, width, carry),
                            fin)
        done = done + n_steps * width
    for hh in range(A_HEADS):
        o_ref[:, hh * HEAD_DIM:(hh + 1) * HEAD_DIM] = fin[hh][2] / fin[hh][1]


def _dsa_prompt(qrot, qe, kw, kk, kvb, nb, lp, k_top):
    m = qrot.shape[0]
    tq = DSA_Q_TILE
    nq = lp // tq
    qrow = lambda width: pl.BlockSpec((tq, width), lambda b, i: (b * nq + i, 0))
    return pl.pallas_call(
        functools.partial(_dsa_kernel, k_top=k_top),
        grid=(nb, nq),
        in_specs=[qrow(A_WIDTH), qrow(IDX_HEADS * LANES), qrow(LANES),
                  pl.BlockSpec((lp, LANES), lambda b, i: (b, 0)),
                  pl.BlockSpec((1, lp, A_KV * HEAD_DIM), lambda b, i: (b, 0, 4)),
                  pl.BlockSpec((1, lp, A_KV * HEAD_DIM), lambda b, i: (b, 0, 5))],
        out_specs=qrow(A_WIDTH),
        out_shape=jax.ShapeDtypeStruct((m, A_WIDTH), F32),
        scratch_shapes=[pltpu.VMEM((lp // KEY_CHUNK, tq, KEY_CHUNK), F32)],
        compiler_params=_params("arbitrary", "arbitrary"),
        name="dsa_prompt",
    )(qrot, qe, kw, kk, kvb, kvb)


def _lambda_full(lq_ref, lam_init):
    lq = lq_ref[...]
    a = jnp.sum(lq[0:1] * lq[1:2], axis=1, keepdims=True)
    b = jnp.sum(lq[2:3] * lq[3:4], axis=1, keepdims=True)
    return jnp.exp(a) - jnp.exp(b) + lam_init


def _subln(o, g, lam_init):
    ms = jnp.mean(o * o, axis=-1, keepdims=True)
    return (o * lax.rsqrt(ms + EPS) * g) * (1.0 - lam_init)


def _diff_kernel(qb_ref, kb_ref, vb_ref, lq_ref, g_ref, o_ref, *, lam_init):
    tq = qb_ref.shape[0]
    kc = KEY_CHUNK
    i = pl.program_id(1)
    q0 = i * tq
    n_full = q0 // kc
    n_all = (q0 + tq + kc - 1) // kc
    lam = _lambda_full(lq_ref, lam_init)
    scale2 = HEAD_DIM ** -0.5 * LOG2_E
    rows = B_GROUP * tq
    row_pos = q0 + lax.broadcasted_iota(I32, (tq, kc), 0)
    col_iota = lax.broadcasted_iota(I32, (tq, kc), 1)
    row_pos = jnp.concatenate([row_pos] * B_GROUP, axis=0)
    col_iota = jnp.concatenate([col_iota] * B_GROUP, axis=0)

    combos = [(g, c2) for g in range(B_KV) for c2 in range(2)]

    def step(c0, carry, width, masked):
        k0 = pl.multiple_of(c0 * kc, kc)
        out = []
        for idx, (g, c2) in enumerate(combos):
            m, l, acc = carry[idx]
            qgc = jnp.concatenate(
                [qb_ref[:, ((g * B_GROUP + r) * 2 + c2) * HEAD_DIM:((g * B_GROUP + r) * 2 + c2 + 1) * HEAD_DIM]
                 for r in range(B_GROUP)], axis=0)
            kcol = (g * 2 + c2) * HEAD_DIM
            kch = kb_ref[0, pl.ds(k0, width * kc), kcol:kcol + HEAD_DIM]
            vch = vb_ref[0, pl.ds(k0, width * kc), g * B_VDIM:(g + 1) * B_VDIM]
            s = lax.dot_general(qgc, kch, _NT, preferred_element_type=F32) * scale2
            if masked:
                s = jnp.where(k0 + col_iota <= row_pos, s, NEG_BIG)
            m_new = jnp.maximum(m, jnp.max(s, axis=1, keepdims=True))
            p = jnp.exp2(s - m_new)
            alpha = jnp.exp2(m - m_new)
            l = alpha * l + jnp.sum(p, axis=1, keepdims=True)
            acc = alpha * acc + jnp.dot(p.astype(BF16), vch, preferred_element_type=F32)
            out.append((m_new, l, acc))
        return tuple(out)

    init = tuple((jnp.full((rows, 1), NEG_BIG, F32), jnp.zeros((rows, 1), F32),
                  jnp.zeros((rows, B_VDIM), F32)) for _ in combos)
    carry, done = init, 0
    for width in ATTN_STEP_WIDTHS:
        n_steps = (n_full - done) // width
        carry = lax.fori_loop(0, n_steps, lambda j, cr, done=done, width=width: step(done + width * j, cr, width, False),
                              carry)
        done = done + n_steps * width
    carry = lax.fori_loop(n_full, n_all, lambda c, cr: step(c, cr, 1, True), carry)
    for g in range(B_KV):
        o0, o1 = (carry[g * 2 + c2][2] / carry[g * 2 + c2][1] for c2 in range(2))
        o = _subln(o0 - lam * o1, g_ref[...], lam_init)
        for r in range(B_GROUP):
            hh = g * B_GROUP + r
            o_ref[:, hh * B_VDIM:(hh + 1) * B_VDIM] = o[r * tq:(r + 1) * tq]


def _diff_prompt(qrot, kvb, lam_qk, g_sub, nb, lp, lam_init):
    m = qrot.shape[0]
    tq = DIFF_Q_TILE
    nq = lp // tq
    return pl.pallas_call(
        functools.partial(_diff_kernel, lam_init=lam_init),
        grid=(nb, nq),
        in_specs=[pl.BlockSpec((tq, B_HEADS * 2 * HEAD_DIM), lambda b, i: (b * nq + i, 1)),
                  pl.BlockSpec((1, lp, B_KV * 2 * HEAD_DIM), lambda b, i: (b, 0, 0)),
                  pl.BlockSpec((1, lp, B_KV * B_VDIM), lambda b, i: (b, 0, 1)),
                  pl.BlockSpec((4, HEAD_DIM), lambda b, i: (0, 0)),
                  pl.BlockSpec((1, B_VDIM), lambda b, i: (0, 0))],
        out_specs=pl.BlockSpec((tq, B_WIDTH), lambda b, i: (b * nq + i, 0)),
        out_shape=jax.ShapeDtypeStruct((m, B_WIDTH), F32),
        compiler_params=_params("arbitrary", "arbitrary"),
        name="diff_prompt",
    )(qrot, kvb, kvb, lam_qk, g_sub)


def _sigmoid(x):
    return 1.0 / (1.0 + jnp.exp(-x))


def _merge_kernel(x_ref, oa_ref, ob_ref, za_ref, zb_ref, ga_ref, gb_ref, wa_ref, wb_ref, wo_ref, g_ref,
                  x_out_ref, n_out_ref):
    za, zb = za_ref[...], zb_ref[...]
    ya = (oa_ref[...] * (za * _sigmoid(za))).astype(BF16)
    yb = (ob_ref[...] * (zb * _sigmoid(zb))).astype(BF16)
    branch_a = jnp.dot(ya, wa_ref[...], preferred_element_type=F32)
    branch_b = jnp.dot(yb, wb_ref[...], preferred_element_type=F32)
    mix = _sigmoid(ga_ref[...]) * branch_a + _sigmoid(gb_ref[...]) * branch_b
    x_new = x_ref[...] + jnp.dot(mix.astype(BF16), wo_ref[...], preferred_element_type=F32)
    x_out_ref[...] = x_new
    ms = jnp.mean(x_new * x_new, axis=-1, keepdims=True)
    n_out_ref[...] = (x_new * lax.rsqrt(ms + EPS) * g_ref[...]).astype(n_out_ref.dtype)


def _merge(x, o_a, o_b, zg, w_pa, w_pb, w_o, g_next, norm_dtype):
    m, d = x.shape
    tm = min(MERGE_TILE, m)
    row = lambda width, col: pl.BlockSpec((tm, width), lambda i: (i, col))
    full = lambda a: pl.BlockSpec(a.shape, lambda i: (0, 0), pipeline_mode=pl.Buffered(1))
    return pl.pallas_call(
        _merge_kernel,
        grid=(m // tm,),
        in_specs=[row(d, 0), row(A_WIDTH, 0), row(B_WIDTH, 0),
                  row(A_WIDTH, 0), row(B_WIDTH, 1), row(d, 1), row(d, 2),
                  full(w_pa), full(w_pb), full(w_o), pl.BlockSpec((1, d), lambda i: (0, 0))],
        out_specs=[row(d, 0), row(d, 0)],
        out_shape=[jax.ShapeDtypeStruct((m, d), F32), jax.ShapeDtypeStruct((m, d), norm_dtype)],
        compiler_params=_params("arbitrary"),
        name="merge",
    )(x, o_a, o_b, zg, zg, zg, zg, w_pa, w_pb, w_o, g_next.reshape(1, d).astype(F32))


def _decode_kernel(pt_ref, qi_ref, w_ref, qa_ref, qb_ref, kin_ref, kan_ref, van_ref, kbn_ref, vbn_ref,
                   lq_ref, g_ref, cki_ref, cka_ref, cva_ref, ckb_ref, cvb_ref,
                   oa_ref, ob_ref, bki, bka, bva, bkb, bvb, sems, *, layer, past_len, n_new, k_top, lam_init):
    b = pl.program_id(0)
    nb = pl.num_programs(0)
    n_pages = bki.shape[1] - 1
    page = bki.shape[3]
    s_all = (n_pages + 1) * page
    pairs = ((cki_ref, bki), (cka_ref, bka), (cva_ref, bva), (ckb_ref, bkb), (cvb_ref, bvb))

    def page_copy(src, dst, which, seq, slot, p):
        return pltpu.make_async_copy(src.at[layer, pt_ref[seq, p]], dst.at[slot, p], sems.at[slot, which])

    def start_all(seq, slot):
        for p in range(n_pages):
            for which, (src, dst) in enumerate(pairs):
                page_copy(src, dst, which, seq, slot, p).start(priority=p % 2)

    @pl.when(b == 0)
    def _():
        start_all(0, 0)

    slot = lax.rem(b, 2)

    @pl.when(b + 1 < nb)
    def _():
        start_all(b + 1, 1 - slot)

    for buf, new in ((bki, kin_ref), (bka, kan_ref), (bva, van_ref), (bkb, kbn_ref), (bvb, vbn_ref)):
        buf[slot, n_pages] = jnp.zeros(buf.shape[2:], F32)
        buf[slot, n_pages, 0:new.shape[1]] = new[0]

    for p in range(n_pages):
        for which, (src, dst) in enumerate(pairs):
            page_copy(src, dst, which, b, slot, p).wait()

    rows = DEC_ROWS
    col = lax.broadcasted_iota(I32, (rows, s_all), 1)
    tok = jnp.minimum(lax.broadcasted_iota(I32, (rows, s_all), 0), n_new - 1)
    causal = (col <= past_len + tok) & (col < past_len + n_new)

    kidx_t = jnp.concatenate([bki[slot, p] for p in range(n_pages + 1)], axis=1).astype(BF16)
    d = jnp.dot(qi_ref[0], kidx_t, preferred_element_type=F32)
    w_idx = w_ref[0] * (IDX_DIM ** -0.5)
    acc = jnp.zeros((rows, s_all), F32)
    for h in range(IDX_HEADS):
        acc = acc + jnp.maximum(d[h * rows:(h + 1) * rows], 0.0) * w_idx[:, h:h + 1]
    score = jnp.where(causal, _one_zero(acc), jnp.nan)

    def count_ge(value):
        return jnp.sum(jnp.where(score >= value, 1.0, 0.0), axis=1, keepdims=True)

    scale = HEAD_DIM ** -0.5

    def softmax_pv(s, v):
        m = jnp.max(s, axis=1, keepdims=True)
        p = jnp.exp(s - m)
        l = jnp.sum(p, axis=1, keepdims=True)
        return jnp.dot(p.astype(v.dtype), v, preferred_element_type=F32) / l

    lam = _lambda_full(lq_ref, lam_init)
    causal_bias = jnp.where(causal, 0.0, NEG_BIG)
    bias_b = jnp.concatenate([causal_bias] * B_GROUP, axis=0)
    for g in range(B_KV):
        vg = jnp.concatenate(
            [bvb[slot, :, pl.ds(hf * B_KV + g, page, stride=2 * B_KV), :].reshape(s_all, HEAD_DIM)
             for hf in range(2)], axis=1).astype(BF16)
        outs = []
        for c2 in range(2):
            kgc = bkb[slot, :, pl.ds(g * 2 + c2, page, stride=2 * B_KV), :].reshape(s_all, HEAD_DIM).astype(BF16)
            s = lax.dot_general(qb_ref[0, g, c2], kgc, _NT, preferred_element_type=F32) * scale + bias_b
            outs.append(softmax_pv(s, vg))
        o = _subln(outs[0] - lam * outs[1], g_ref[...], lam_init)
        for r in range(B_GROUP):
            hh = g * B_GROUP + r
            ob_ref[0, :, hh * B_VDIM:(hh + 1) * B_VDIM] = o[r * rows:(r + 1) * rows]

    thr_key = _search_threshold(lambda key: count_ge(_key_value(key)), rows, float(k_top),
                                bits=DEC_SEARCH_BITS, unroll=True)
    thr = _threshold_value(thr_key)
    c_ge = count_ge(thr)
    c_gt = jnp.sum(jnp.where(score > thr, 1.0, 0.0), axis=1, keepdims=True)
    need = float(k_top) - c_gt
    sel = (score >= thr) & (score > -jnp.inf) & (score < jnp.inf)
    sel_bias = jnp.where(sel, 0.0, NEG_BIG)

    def drop_ties(sel_bias):
        eq = score == thr
        tri = _strict_upper(LANES)
        run = jnp.zeros((rows, 1), F32)
        drops = []
        for j in range(s_all // LANES):
            eqj = jnp.where(eq[:, j * LANES:(j + 1) * LANES], 1.0, 0.0)
            before = jnp.dot(eqj.astype(BF16), tri, preferred_element_type=F32) + run
            drops.append(before >= need)
            run = run + jnp.sum(eqj, axis=1, keepdims=True)
        drop = eq & jnp.concatenate(drops, axis=1) & (c_ge > float(k_top))
        return jnp.where(drop, NEG_BIG, sel_bias)

    tie_rows = jnp.where((c_ge > float(k_top)) & (thr_key > jnp.int32(INT_MIN)), 1.0, 0.0)
    sel_bias = lax.cond(jnp.max(tie_rows) > 0.0, drop_ties, lambda x: x, sel_bias)

    bias_a = jnp.concatenate([sel_bias] * A_GROUP, axis=0)
    for g in range(A_KV):
        kg = bka[slot, :, pl.ds(g, page, stride=A_KV), :].reshape(s_all, HEAD_DIM).astype(BF16)
        vg = bva[slot, :, pl.ds(g, page, stride=A_KV), :].reshape(s_all, HEAD_DIM).astype(BF16)
        s = lax.dot_general(qa_ref[0, g], kg, _NT, preferred_element_type=F32) * scale + bias_a
        o = softmax_pv(s, vg)
        for r in range(A_GROUP):
            hh = g * A_GROUP + r
            oa_ref[0, :, hh * HEAD_DIM:(hh + 1) * HEAD_DIM] = o[r * rows:(r + 1) * rows]


def _decode_attend(page_table, qi8, w8, qa8, qb8, new_rows, lam_qk, g_sub, caches, layer, past_len, n_new,
                   k_top, lam_init):
    nseq, n_pages = page_table.shape
    page = caches[0].shape[2]
    seq_spec = lambda a: pl.BlockSpec((1,) + a.shape[1:], lambda b, pt: (b,) + (0,) * (a.ndim - 1))
    const = lambda a: pl.BlockSpec(a.shape, lambda b, pt: (0,) * a.ndim)
    any_spec = pl.BlockSpec(memory_space=pl.ANY)
    per_seq = [qi8, w8, qa8, qb8] + list(new_rows)
    buf = lambda c: pltpu.VMEM((2, n_pages + 1) + c.shape[2:], F32)
    grid_spec = pltpu.PrefetchScalarGridSpec(
        num_scalar_prefetch=1,
        grid=(nseq,),
        in_specs=[seq_spec(a) for a in per_seq] + [const(lam_qk), const(g_sub)] + [any_spec] * 5,
        out_specs=[pl.BlockSpec((1, DEC_ROWS, A_WIDTH), lambda b, pt: (b, 0, 0)),
                   pl.BlockSpec((1, DEC_ROWS, B_WIDTH), lambda b, pt: (b, 0, 0))],
        scratch_shapes=[buf(c) for c in caches] + [pltpu.SemaphoreType.DMA((2, 5))],
    )
    return pl.pallas_call(
        functools.partial(_decode_kernel, layer=layer, past_len=past_len, n_new=n_new, k_top=k_top,
                          lam_init=lam_init),
        grid_spec=grid_spec,
        out_shape=[jax.ShapeDtypeStruct((nseq, DEC_ROWS, A_WIDTH), F32),
                   jax.ShapeDtypeStruct((nseq, DEC_ROWS, B_WIDTH), F32)],
        compiler_params=_params("arbitrary"),
        name="decode_attend",
    )(page_table, *per_seq, lam_qk, g_sub, *caches)


def _rope_tables(pos):
    def cs(dim):
        half = dim // 2
        inv = ROPE_THETA ** (-jnp.arange(half, dtype=F32) / half)
        ang = pos.astype(F32)[:, None] * inv[None, :]
        return jnp.cos(ang), jnp.sin(ang)
    cos, sin = cs(HEAD_DIM)
    c128 = jnp.concatenate([cos, cos], axis=1)
    s128 = jnp.concatenate([-sin, sin], axis=1)
    cos, sin = cs(IDX_DIM)
    zero = jnp.zeros_like(sin)
    c64 = jnp.concatenate([cos, cos, cos, cos], axis=1)
    lo64 = jnp.concatenate([-sin, zero, -sin, zero], axis=1)
    hi64 = jnp.concatenate([zero, sin, zero, sin], axis=1)
    n = pos.shape[0]
    tail = jnp.concatenate([jnp.full((n, IDX_HEADS), IDX_HEADS ** -0.5, F32),
                            jnp.zeros((n, LANES - IDX_DIM - IDX_HEADS), F32)], axis=1)
    ck = jnp.concatenate([cos, cos, tail], axis=1)
    zero_tail = jnp.zeros((n, LANES - IDX_DIM), F32)
    klo = jnp.concatenate([-sin, zero, zero_tail], axis=1)
    khi = jnp.concatenate([zero, sin, zero_tail], axis=1)
    return (c128, s128), (c64, lo64, hi64, ck, klo, khi)


def _split_weights(w):
    sizes = (A_WIDTH, A_KV * HEAD_DIM, A_KV * HEAD_DIM, A_WIDTH, IDX_HEADS * IDX_DIM, IDX_DIM, IDX_HEADS,
             B_HEADS * 2 * HEAD_DIM, B_KV * 2 * HEAD_DIM, B_KV * B_VDIM, B_WIDTH, w.shape[0], w.shape[0])
    pts = np.cumsum(sizes)[:-1]
    q_a, k_a, v_a, z_a, qi, ki, wi, q_b, k_b, v_b, z_b, g_a, g_b = jnp.split(w, pts, axis=1)
    pad = jnp.zeros((w.shape[0], LANES - IDX_DIM - IDX_HEADS), w.dtype)
    cat = lambda parts: jnp.concatenate(parts, axis=1).astype(BF16)
    return (cat([q_a, q_b]), cat([k_b, v_b, k_a, v_a]), cat([qi, ki, wi, pad]), cat([z_a, z_b, g_a, g_b]))


def _project_all(h, weights, tabs, nb, lp, n_keep, expand):
    w_q, w_kv, w_idx, w_zg = weights
    qrot = _proj_q(h, w_q, tabs[0])
    kv = _proj_kv(h.reshape(nb, lp, h.shape[1]), w_kv, tabs[0], n_keep)
    idx = _proj_idx(h, w_idx, tabs[1], expand)
    zg = _proj_plain(h, w_zg, GATE_COL_TILE)
    return qrot, kv, idx, zg


def kernel(x_prompt, x_sample, cache_k_a, cache_v_a, cache_k_idx, cache_k_b, cache_v_b, page_table,
           meta_tokens, norm_g, w_in, lam_qk, subln_g, w_proj_a, w_proj_b, w_out, final_norm_g):
    nb, seq, d = x_prompt.shape
    nseq, n_new, _ = x_sample.shape
    depth = norm_g.shape[0]
    n_meta = meta_tokens.shape[0]
    n_pages = page_table.shape[1]
    page = cache_k_a.shape[2]
    past_len = n_pages * page
    n_tok = n_meta + seq
    lp = -(-n_tok // ROW_TILE) * ROW_TILE
    k_top_p = min(TOP_K_MAX, seq // 4)
    k_top_s = min(TOP_K_MAX, (past_len + n_new) // 4)

    meta = jnp.broadcast_to(meta_tokens.astype(x_prompt.dtype)[None], (nb, n_meta, d))
    xp = jnp.concatenate([meta, x_prompt, jnp.zeros((nb, lp - n_tok, d), x_prompt.dtype)], axis=1)
    xp = xp.reshape(nb * lp, d)
    xs = x_sample.reshape(nseq * n_new, d)
    tabs_p = _rope_tables(jnp.arange(lp, dtype=I32))
    tabs_s = _rope_tables(past_len + jnp.arange(nseq * n_new, dtype=I32) % n_new)

    np_, pg = cache_k_a.shape[1], cache_k_a.shape[2]
    caches = (jnp.swapaxes(cache_k_idx, 2, 3),
              cache_k_a.reshape(depth, np_, pg * A_KV, HEAD_DIM),
              cache_v_a.reshape(depth, np_, pg * A_KV, HEAD_DIM),
              cache_k_b.reshape(depth, np_, pg * B_KV * 2, HEAD_DIM),
              cache_v_b.reshape(depth, np_, pg, B_KV, 2, HEAD_DIM).transpose(0, 1, 2, 4, 3, 5)
              .reshape(depth, np_, pg * B_KV * 2, HEAD_DIM))

    hp = _rmsnorm(xp, norm_g[0], BF16)
    hs = _rmsnorm(xs, norm_g[0], BF16)
    rows_p = [[] for _ in range(5)]
    rows_s = [[] for _ in range(5)]
    pad_t = lambda a, axis: jnp.pad(a, [(0, DEC_ROWS - n_new) if ax == axis else (0, 0) for ax in range(a.ndim)])
    for l in range(depth):
        lam_init = _lambda_init(l)
        weights = _split_weights(w_in[l])
        w_pa, w_pb, w_o = w_proj_a[l].astype(BF16), w_proj_b[l].astype(BF16), w_out[l].astype(BF16)
        last = l + 1 == depth
        g_next = final_norm_g if last else norm_g[l + 1]
        norm_dtype = F32 if last else BF16
        lq = lam_qk[l].astype(F32)
        g_sub = subln_g[l].reshape(1, B_VDIM).astype(F32)

        qrot, (kvb, nkb, nvb, nka, nva), (qe, kw, kk), zg = _project_all(hp, weights, tabs_p, nb, lp, n_tok, True)
        o_a = _dsa_prompt(qrot, qe, kw, kk, kvb, nb, lp, k_top_p)
        o_b = _diff_prompt(qrot, kvb, lq, g_sub, nb, lp, lam_init)
        xp, hp = _merge(xp, o_a, o_b, zg, w_pa, w_pb, w_o, g_next, norm_dtype)
        ki_p = kw.reshape(nb, lp, LANES)[:, :n_tok, :IDX_DIM]
        for i, a in enumerate((nka.reshape(nb, n_tok, A_KV, HEAD_DIM), nva.reshape(nb, n_tok, A_KV, HEAD_DIM), ki_p,
                               nkb.reshape(nb, n_tok, B_KV, 2, HEAD_DIM), nvb.reshape(nb, n_tok, B_KV, B_VDIM))):
            rows_p[i].append(a)

        ms = nseq * n_new
        qrot, (_, nkb, nvb, nka, nva), (qi, kw, _), zg = _project_all(hs, weights, tabs_s, 1, ms, ms, False)
        qa8 = pad_t(qrot[:, :A_WIDTH].reshape(nseq, n_new, A_KV, A_GROUP, HEAD_DIM).transpose(0, 2, 3, 1, 4), 3)
        qa8 = qa8.reshape(nseq, A_KV, A_GROUP * DEC_ROWS, HEAD_DIM)
        qb8 = qrot[:, A_WIDTH:].reshape(nseq, n_new, B_KV, B_GROUP, 2, HEAD_DIM).transpose(0, 2, 4, 3, 1, 5)
        qb8 = pad_t(qb8, 4).reshape(nseq, B_KV, 2, B_GROUP * DEC_ROWS, HEAD_DIM)
        qi8 = pad_t(qi.reshape(nseq, n_new, IDX_HEADS, IDX_DIM).transpose(0, 2, 1, 3), 2)
        qi8 = qi8.reshape(nseq, IDX_HEADS * DEC_ROWS, IDX_DIM)
        w8 = pad_t(kw[:, IDX_DIM:IDX_DIM + IDX_HEADS].reshape(nseq, n_new, IDX_HEADS), 1)
        ki_s = kw[:, :IDX_DIM].reshape(nseq, n_new, IDX_DIM)
        new_s = (nka.reshape(nseq, n_new, A_KV, HEAD_DIM), nva.reshape(nseq, n_new, A_KV, HEAD_DIM), ki_s,
                 nkb.reshape(nseq, n_new, B_KV, 2, HEAD_DIM), nvb.reshape(nseq, n_new, B_KV, B_VDIM))
        new_rows = [pad_t(new_s[j], 1) for j in (2, 0, 1, 3, 4)]
        new_rows[1:4] = [a.reshape(nseq, -1, HEAD_DIM) for a in new_rows[1:4]]
        new_rows[0] = jnp.pad(jnp.swapaxes(ki_s, 1, 2), ((0, 0), (0, 0), (0, pg - n_new)))
        new_rows[4] = (new_rows[4].reshape(nseq, DEC_ROWS, B_KV, 2, HEAD_DIM).transpose(0, 1, 3, 2, 4)
                       .reshape(nseq, -1, HEAD_DIM))
        o_a8, o_b8 = _decode_attend(page_table, qi8, w8, qa8, qb8, new_rows, lq, g_sub, caches, l, past_len,
                                    n_new, k_top_s, lam_init)
        o_a = o_a8[:, :n_new].reshape(ms, A_WIDTH)
        o_b = o_b8[:, :n_new].reshape(ms, B_WIDTH)
        xs, hs = _merge(xs, o_a, o_b, zg, w_pa, w_pb, w_o, g_next, norm_dtype)
        for i, a in enumerate(new_s):
            rows_s[i].append(a)

    y_prompt = hp.reshape(nb, lp, d)[:, n_meta:n_tok]
    y_sample = hs.reshape(nseq, n_new, d)
    return (y_prompt, y_sample) + tuple(jnp.stack(r) for r in rows_p) + tuple(jnp.stack(r) for r in rows_s)
```

```python
import functools
import math

import jax
import jax.numpy as jnp
import numpy as np
from jax import lax
from jax.experimental import pallas as pl
from jax.experimental.pallas import tpu as pltpu

HEAD_DIM = 128
A_HEADS = 8
A_KV = 2
A_GROUP = A_HEADS // A_KV
A_WIDTH = A_HEADS * HEAD_DIM
IDX_HEADS = 16
IDX_DIM = 64
B_HEADS = 4
B_KV = 2
B_GROUP = B_HEADS // B_KV
B_VDIM = 2 * HEAD_DIM
B_WIDTH = B_HEADS * B_VDIM
TOP_K_MAX = 256
ROPE_THETA = 10000.0
EPS = 1e-6

LANES = 128
ROW_TILE = 256
GATE_ROW_TILE = 512
GATE_COL_TILE = 2048
KEY_CHUNK = 256
ATTN_STEP_WIDTHS = (4, 2, 1)
DSA_Q_TILE = 256
SEARCH_ROWS = 128
SCORE_ROWS = 64
DIFF_Q_TILE = 256
MERGE_TILE = 256
DEC_ROWS = 8
DEC_SEARCH_BITS = 3
VMEM_LIMIT = 48 * 1024 * 1024

LOG2_E = 1.4426950408889634
NEG_BIG = -1e30
INT_MIN = -2 ** 31
KEY_NEG_INF = -2139095041

F32 = jnp.float32
BF16 = jnp.bfloat16
I32 = jnp.int32

_NT = (((1,), (1,)), ((), ()))


def _params(*sem):
    return pltpu.CompilerParams(dimension_semantics=sem, vmem_limit_bytes=VMEM_LIMIT)


def _lambda_init(layer):
    return 0.8 - 0.6 * math.exp(-0.3 * layer)


def _rmsnorm_kernel(x_ref, g_ref, o_ref):
    x = x_ref[...]
    ms = jnp.mean(x * x, axis=-1, keepdims=True)
    o_ref[...] = (x * lax.rsqrt(ms + EPS) * g_ref[...]).astype(o_ref.dtype)


def _rmsnorm(x, g, out_dtype):
    m, d = x.shape
    tm = min(ROW_TILE, m)
    return pl.pallas_call(
        _rmsnorm_kernel,
        grid=(m // tm,),
        in_specs=[pl.BlockSpec((tm, d), lambda i: (i, 0)),
                  pl.BlockSpec((1, d), lambda i: (0, 0))],
        out_specs=pl.BlockSpec((tm, d), lambda i: (i, 0)),
        out_shape=jax.ShapeDtypeStruct((m, d), out_dtype),
        compiler_params=_params("arbitrary"),
        name="rmsnorm",
    )(x, g.reshape(1, d).astype(F32))


def _rope128(a, c, s):
    return a * c + pltpu.roll(a, HEAD_DIM // 2, 1) * s


def _rope64(a, c, s_lo, s_hi):
    return a * c + pltpu.roll(a, LANES - IDX_DIM // 2, 1) * s_lo + pltpu.roll(a, IDX_DIM // 2, 1) * s_hi


def _proj_q_kernel(h_ref, w_ref, c_ref, s_ref, o_ref):
    acc = jnp.dot(h_ref[...], w_ref[...], preferred_element_type=F32)
    c, s = c_ref[...], s_ref[...]
    for j in range(acc.shape[1] // LANES):
        sl = slice(j * LANES, (j + 1) * LANES)
        o_ref[:, sl] = _rope128(acc[:, sl], c, s).astype(o_ref.dtype)


def _proj_kv_kernel(h_ref, w_ref, c_ref, s_ref, kv_ref, kb_ref, vb_ref, ka_ref, va_ref):
    acc = jnp.dot(h_ref[0], w_ref[...], preferred_element_type=F32)
    c, s = c_ref[...], s_ref[...]
    outs = ((kb_ref, 0, 4, True), (vb_ref, 4, 4, False), (ka_ref, 8, 2, True), (va_ref, 10, 2, False))
    for ref, start, count, rot in outs:
        for j in range(count):
            a = acc[:, (start + j) * LANES:(start + j + 1) * LANES]
            if rot:
                a = _rope128(a, c, s)
            ref[0, :, j * LANES:(j + 1) * LANES] = a
            kv_ref[0, :, (start + j) * LANES:(start + j + 1) * LANES] = a.astype(kv_ref.dtype)


def _proj_idx_kernel(h_ref, w_ref, c_ref, slo_ref, shi_ref, ck_ref, klo_ref, khi_ref,
                     q_ref, kw_ref, kk_ref, *, expand):
    acc = jnp.dot(h_ref[...], w_ref[...], preferred_element_type=F32)
    c, slo, shi = c_ref[...], slo_ref[...], shi_ref[...]
    lane = lax.broadcasted_iota(I32, (acc.shape[0], LANES), 1)
    low = lane < IDX_DIM
    for j in range(IDX_HEADS // 2):
        a = _rope64(acc[:, j * LANES:(j + 1) * LANES], c, slo, shi)
        if expand:
            q_ref[:, (2 * j) * LANES:(2 * j + 1) * LANES] = jnp.where(low, a, 0.0).astype(q_ref.dtype)
            q_ref[:, (2 * j + 1) * LANES:(2 * j + 2) * LANES] = jnp.where(low, 0.0, a).astype(q_ref.dtype)
        else:
            q_ref[:, j * LANES:(j + 1) * LANES] = a.astype(q_ref.dtype)
    kw = _rope64(acc[:, IDX_HEADS * IDX_DIM:], ck_ref[...], klo_ref[...], khi_ref[...])
    kw_ref[...] = kw
    kk_ref[...] = jnp.where(low, kw, pltpu.roll(kw, IDX_DIM, 1)).astype(kk_ref.dtype)


def _proj_plain_kernel(h_ref, w_ref, o_ref):
    o_ref[...] = jnp.dot(h_ref[...], w_ref[...], preferred_element_type=F32).astype(o_ref.dtype)


def _row_spec(tm, width, ntab=None):
    if ntab is None:
        return pl.BlockSpec((tm, width), lambda j, i: (i, 0))
    return pl.BlockSpec((tm, width), lambda j, i: (i % ntab, 0))


def _proj_q(h, w, tabs):
    m, d = h.shape
    n = w.shape[1]
    tm = min(ROW_TILE, m)
    ntab = tabs[0].shape[0] // tm
    return pl.pallas_call(
        _proj_q_kernel,
        grid=(1, m // tm),
        in_specs=[_row_spec(tm, d), pl.BlockSpec((d, n), lambda j, i: (0, 0)),
                  _row_spec(tm, LANES, ntab), _row_spec(tm, LANES, ntab)],
        out_specs=_row_spec(tm, n),
        out_shape=jax.ShapeDtypeStruct((m, n), BF16),
        compiler_params=_params("arbitrary", "arbitrary"),
        name="proj_q",
    )(h, w, tabs[0], tabs[1])


def _proj_kv(h3, w, tabs, n_keep):
    nb, lp, d = h3.shape
    n = w.shape[1]
    tm = min(DIFF_Q_TILE, lp)
    row = lambda width: pl.BlockSpec((1, tm, width), lambda b, i: (b, i, 0))
    tab = pl.BlockSpec((tm, LANES), lambda b, i: (i, 0))
    f32_out = lambda width: jax.ShapeDtypeStruct((nb, n_keep, width), F32)
    return pl.pallas_call(
        _proj_kv_kernel,
        grid=(nb, lp // tm),
        in_specs=[row(d), pl.BlockSpec((d, n), lambda b, i: (0, 0)), tab, tab],
        out_specs=[row(n), row(4 * LANES), row(4 * LANES), row(2 * LANES), row(2 * LANES)],
        out_shape=[jax.ShapeDtypeStruct((nb, lp, n), BF16),
                   f32_out(4 * LANES), f32_out(4 * LANES), f32_out(2 * LANES), f32_out(2 * LANES)],
        compiler_params=_params("arbitrary", "arbitrary"),
        name="proj_kv",
    )(h3, w, tabs[0], tabs[1])


def _proj_idx(h, w, tabs, expand):
    m, d = h.shape
    n = w.shape[1]
    tm = min(ROW_TILE, m)
    ntab = tabs[0].shape[0] // tm
    qw = IDX_HEADS * (LANES if expand else IDX_DIM)
    return pl.pallas_call(
        functools.partial(_proj_idx_kernel, expand=expand),
        grid=(1, m // tm),
        in_specs=[_row_spec(tm, d), pl.BlockSpec((d, n), lambda j, i: (0, 0))]
                 + [_row_spec(tm, LANES, ntab)] * 6,
        out_specs=[_row_spec(tm, qw), _row_spec(tm, LANES), _row_spec(tm, LANES)],
        out_shape=[jax.ShapeDtypeStruct((m, qw), BF16),
                   jax.ShapeDtypeStruct((m, LANES), F32),
                   jax.ShapeDtypeStruct((m, LANES), BF16)],
        compiler_params=_params("arbitrary", "arbitrary"),
        name="proj_idx",
    )(h, w, *tabs)


def _proj_plain(h, w, tn):
    m, d = h.shape
    n = w.shape[1]
    tm = GATE_ROW_TILE if m % GATE_ROW_TILE == 0 else min(ROW_TILE, m)
    return pl.pallas_call(
        _proj_plain_kernel,
        grid=(n // tn, m // tm),
        in_specs=[_row_spec(tm, d), pl.BlockSpec((d, tn), lambda j, i: (0, j))],
        out_specs=pl.BlockSpec((tm, tn), lambda j, i: (i, j)),
        out_shape=jax.ShapeDtypeStruct((m, n), F32),
        compiler_params=_params("arbitrary", "arbitrary"),
        name="proj_gate",
    )(h, w)


def _key_value(key):
    return lax.bitcast_convert_type(key ^ ((key >> 31) & jnp.int32(0x7FFFFFFF)), F32)


def _one_zero(x):
    return jnp.where(x == 0.0, 0.0, x)


def _threshold_value(thr_key):
    return jnp.where(thr_key < jnp.int32(KEY_NEG_INF), -jnp.inf, _key_value(thr_key))


def _search_threshold(count_fns, rows, k_top, bits=1, unroll=False):
    zero = jnp.zeros((rows, 1), I32)
    total = lambda partial: jnp.sum(partial, axis=1, keepdims=True)
    partials = [fn(zero) for fn in count_fns]
    ts = tuple(jnp.where(total(p) >= k_top, zero, jnp.int32(INT_MIN)) for p in partials)
    n_bits = 31

    def refine(shift, nbits, ts):
        cands = [[t | jnp.left_shift(jnp.int32(j), shift) for j in range(1, 2 ** nbits)] for t in ts]
        partials = [[fn(c) for c in cs] for fn, cs in zip(count_fns, cands)]
        out = []
        for t, ps in zip(ts, partials):
            group = jnp.zeros((rows, 1), I32)
            for p in ps:
                group = group + jnp.where(total(p) >= k_top, 1, 0)
            out.append(t | jnp.left_shift(group, shift))
        return tuple(out)

    n_groups, rest = divmod(n_bits, bits)
    if unroll:
        for it in range(n_groups):
            ts = refine(n_bits - bits * (it + 1), bits, ts)
    else:
        ts = lax.fori_loop(0, n_groups, lambda it, ts: refine(n_bits - bits * (it + 1), bits, ts), ts)
    return list(refine(0, rest, ts) if rest else ts)


def _strict_upper(n):
    r = lax.broadcasted_iota(I32, (n, n), 0)
    c = lax.broadcasted_iota(I32, (n, n), 1)
    return jnp.where(r < c, 1.0, 0.0).astype(BF16)


def _dsa_kernel(qa_ref, qe_ref, kw_ref, kk_ref, ka_ref, va_ref, o_ref, sc_ref, *, k_top):
    tq = qa_ref.shape[0]
    kc = KEY_CHUNK
    hr = SEARCH_ROWS
    sr = SCORE_ROWS
    i = pl.program_id(1)
    q0 = i * tq
    nkc = (q0 + tq + kc - 1) // kc
    col_iota = lax.broadcasted_iota(I32, (sr, kc), 1)
    row_iota = lax.broadcasted_iota(I32, (sr, kc), 0)

    slabs = list(range(0, tq, sr))

    def head_dots(c, r0):
        q_stack = jnp.concatenate([qe_ref[r0:r0 + sr, h * LANES:(h + 1) * LANES] for h in range(IDX_HEADS)],
                                  axis=0)
        k0 = pl.multiple_of(c * kc, kc)
        return lax.dot_general(q_stack, kk_ref[pl.ds(k0, kc), :], _NT, preferred_element_type=F32)

    w_cols = {}
    for r0 in slabs:
        w_idx = kw_ref[r0:r0 + sr, IDX_DIM:IDX_DIM + IDX_HEADS] * (IDX_DIM ** -0.5)
        w_cols[r0] = [jnp.broadcast_to(w_idx[:, h:h + 1], (sr, LANES)) for h in range(IDX_HEADS)]

    def slab_keys(d, r0):
        parts = []
        for j in range(kc // LANES):
            acc = jnp.zeros((sr, LANES), F32)
            for h in range(IDX_HEADS):
                acc = acc + jnp.maximum(d[h * sr:(h + 1) * sr, j * LANES:(j + 1) * LANES], 0.0) * w_cols[r0][h]
            parts.append(acc)
        return _one_zero(jnp.concatenate(parts, axis=1))

    def p1_body(c, carry):
        for r0 in slabs:
            sc_ref[c, r0:r0 + sr, :] = slab_keys(head_dots(c, r0), r0)
        return carry
    lax.fori_loop(0, nkc - 1, p1_body, 0)
    for r0 in slabs:
        causal = (nkc - 1) * kc + col_iota <= q0 + r0 + row_iota
        sc_ref[nkc - 1, r0:r0 + sr, :] = jnp.where(causal, slab_keys(head_dots(nkc - 1, r0), r0), jnp.nan)

    search_slabs = list(range(0, tq, hr))

    def count_cmp(cand, strict, r0):
        cand_b = jnp.broadcast_to(cand, (hr, LANES))
        def body(c, acc):
            for j in range(kc // LANES):
                sj = sc_ref[c, r0:r0 + hr, j * LANES:(j + 1) * LANES]
                hit = (sj > cand_b) if strict else (sj >= cand_b)
                acc = acc + jnp.where(hit, 1.0, 0.0)
            return acc
        return lax.fori_loop(0, nkc, body, jnp.zeros((hr, LANES), F32))

    thr_keys = _search_threshold(
        [functools.partial(lambda key, r0: count_cmp(_key_value(key), False, r0), r0=r0) for r0 in search_slabs],
        hr, float(k_top))
    thr_slabs = [_threshold_value(k) for k in thr_keys]
    p_ges = [count_cmp(thr, False, r0) for thr, r0 in zip(thr_slabs, search_slabs)]
    p_gts = [count_cmp(thr, True, r0) for thr, r0 in zip(thr_slabs, search_slabs)]
    for r0, thr_key, thr, p_ge, p_gt in zip(search_slabs, thr_keys, thr_slabs, p_ges, p_gts):
        c_ge = jnp.sum(p_ge, axis=1, keepdims=True)
        need = float(k_top) - jnp.sum(p_gt, axis=1, keepdims=True)
        tie_rows = jnp.where((c_ge > float(k_top)) & (thr_key > jnp.int32(INT_MIN)), 1.0, 0.0)

        @pl.when(jnp.max(tie_rows) > 0.0)
        def _(r0=r0, thr=thr, need=need):
            tri = _strict_upper(kc)
            def body(c, run):
                s = sc_ref[c, r0:r0 + hr, :]
                eq = s == thr
                eqf = jnp.where(eq, 1.0, 0.0)
                before = jnp.dot(eqf.astype(BF16), tri, preferred_element_type=F32) + run
                sc_ref[c, r0:r0 + hr, :] = jnp.where(eq & (before >= need), jnp.nan, s)
                return run + jnp.sum(eqf, axis=1, keepdims=True)
            lax.fori_loop(0, nkc, body, jnp.zeros((hr, 1), F32))

    thr = jnp.concatenate(thr_slabs, axis=0)

    scale2 = HEAD_DIM ** -0.5 * LOG2_E

    def attend(c0, width, carry):
        k0 = pl.multiple_of(c0 * kc, kc)
        score = jnp.concatenate([sc_ref[c0 + u] for u in range(width)], axis=1)
        sel = (score >= thr) & (score > -jnp.inf) & (score < jnp.inf)
        bias = jnp.where(sel, 0.0, NEG_BIG)
        out = []
        for g in range(A_KV):
            qg = jnp.concatenate([qa_ref[:, (g * A_GROUP + r) * HEAD_DIM:(g * A_GROUP + r + 1) * HEAD_DIM]
                                  for r in range(A_GROUP)], axis=0)
            kch = ka_ref[0, pl.ds(k0, width * kc), g * HEAD_DIM:(g + 1) * HEAD_DIM]
            vch = va_ref[0, pl.ds(k0, width * kc), g * HEAD_DIM:(g + 1) * HEAD_DIM]
            s = lax.dot_general(qg, kch, _NT, preferred_element_type=F32)
            ps, stats = [], []
            for r in range(A_GROUP):
                m, l, _ = carry[g * A_GROUP + r]
                sr = s[r * tq:(r + 1) * tq] * scale2 + bias
                m_new = jnp.maximum(m, jnp.max(sr, axis=1, keepdims=True))
                p = jnp.exp2(sr - m_new)
                alpha = jnp.exp2(m - m_new)
                stats.append((m_new, alpha * l + jnp.sum(p, axis=1, keepdims=True), alpha))
                ps.append(p.astype(BF16))
            pv = jnp.dot(jnp.concatenate(ps, axis=0), vch, preferred_element_type=F32)
            for r in range(A_GROUP):
                m_new, l_new, alpha = stats[r]
                out.append((m_new, l_new, alpha * carry[g * A_GROUP + r][2] + pv[r * tq:(r + 1) * tq]))
        return tuple(out)

    init = tuple((jnp.full((tq, 1), NEG_BIG, F32), jnp.zeros((tq, 1), F32),
                  jnp.zeros((tq, HEAD_DIM), F32)) for _ in range(A_HEADS))
    fin, done = init, 0
    for width in ATTN_STEP_WIDTHS:
        n_steps = (nkc - done) // width
        fin = lax.fori_loop(0, n_steps, lambda j, carry, done=done, width=width: attend(done + width * j, width, carry),
                            fin)
        done = done + n_steps * width
    for hh in range(A_HEADS):
        o_ref[:, hh * HEAD_DIM:(hh + 1) * HEAD_DIM] = fin[hh][2] / fin[hh][1]


def _dsa_prompt(qrot, qe, kw, kk, kvb, nb, lp, k_top):
    m = qrot.shape[0]
    tq = DSA_Q_TILE
    nq = lp // tq
    qrow = lambda width: pl.BlockSpec((tq, width), lambda b, i: (b * nq + i, 0))
    return pl.pallas_call(
        functools.partial(_dsa_kernel, k_top=k_top),
        grid=(nb, nq),
        in_specs=[qrow(A_WIDTH), qrow(IDX_HEADS * LANES), qrow(LANES),
                  pl.BlockSpec((lp, LANES), lambda b, i: (b, 0)),
                  pl.BlockSpec((1, lp, A_KV * HEAD_DIM), lambda b, i: (b, 0, 4)),
                  pl.BlockSpec((1, lp, A_KV * HEAD_DIM), lambda b, i: (b, 0, 5))],
        out_specs=qrow(A_WIDTH),
        out_shape=jax.ShapeDtypeStruct((m, A_WIDTH), F32),
        scratch_shapes=[pltpu.VMEM((lp // KEY_CHUNK, tq, KEY_CHUNK), F32)],
        compiler_params=_params("arbitrary", "arbitrary"),
        name="dsa_prompt",
    )(qrot, qe, kw, kk, kvb, kvb)


def _lambda_full(lq_ref, lam_init):
    lq = lq_ref[...]
    a = jnp.sum(lq[0:1] * lq[1:2], axis=1, keepdims=True)
    b = jnp.sum(lq[2:3] * lq[3:4], axis=1, keepdims=True)
    return jnp.exp(a) - jnp.exp(b) + lam_init


def _subln(o, g, lam_init):
    ms = jnp.mean(o * o, axis=-1, keepdims=True)
    return (o * lax.rsqrt(ms + EPS) * g) * (1.0 - lam_init)


def _diff_kernel(qb_ref, kb_ref, vb_ref, lq_ref, g_ref, o_ref, *, lam_init):
    tq = qb_ref.shape[0]
    kc = KEY_CHUNK
    i = pl.program_id(1)
    q0 = i * tq
    n_full = q0 // kc
    n_all = (q0 + tq + kc - 1) // kc
    lam = _lambda_full(lq_ref, lam_init)
    scale2 = HEAD_DIM ** -0.5 * LOG2_E
    rows = B_GROUP * tq
    row_pos = q0 + lax.broadcasted_iota(I32, (tq, kc), 0)
    col_iota = lax.broadcasted_iota(I32, (tq, kc), 1)
    row_pos = jnp.concatenate([row_pos] * B_GROUP, axis=0)
    col_iota = jnp.concatenate([col_iota] * B_GROUP, axis=0)

    combos = [(g, c2) for g in range(B_KV) for c2 in range(2)]

    def step(c0, carry, width, masked):
        k0 = pl.multiple_of(c0 * kc, kc)
        out = []
        for idx, (g, c2) in enumerate(combos):
            m, l, acc = carry[idx]
            qgc = jnp.concatenate(
                [qb_ref[:, ((g * B_GROUP + r) * 2 + c2) * HEAD_DIM:((g * B_GROUP + r) * 2 + c2 + 1) * HEAD_DIM]
                 for r in range(B_GROUP)], axis=0)
            kcol = (g * 2 + c2) * HEAD_DIM
            kch = kb_ref[0, pl.ds(k0, width * kc), kcol:kcol + HEAD_DIM]
            vch = vb_ref[0, pl.ds(k0, width * kc), g * B_VDIM:(g + 1) * B_VDIM]
            s = lax.dot_general(qgc, kch, _NT, preferred_element_type=F32) * scale2
            if masked:
                s = jnp.where(k0 + col_iota <= row_pos, s, NEG_BIG)
            m_new = jnp.maximum(m, jnp.max(s, axis=1, keepdims=True))
            p = jnp.exp2(s - m_new)
            alpha = jnp.exp2(m - m_new)
            l = alpha * l + jnp.sum(p, axis=1, keepdims=True)
            acc = alpha * acc + jnp.dot(p.astype(BF16), vch, preferred_element_type=F32)
            out.append((m_new, l, acc))
        return tuple(out)

    init = tuple((jnp.full((rows, 1), NEG_BIG, F32), jnp.zeros((rows, 1), F32),
                  jnp.zeros((rows, B_VDIM), F32)) for _ in combos)
    carry, done = init, 0
    for width in ATTN_STEP_WIDTHS:
        n_steps = (n_full - done) // width
        carry = lax.fori_loop(0, n_steps, lambda j, cr, done=done, width=width: step(done + width * j, cr, width, False),
                              carry)
        done = done + n_steps * width
    carry = lax.fori_loop(n_full, n_all, lambda c, cr: step(c, cr, 1, True), carry)
    for g in range(B_KV):
        o0, o1 = (carry[g * 2 + c2][2] / carry[g * 2 + c2][1] for c2 in range(2))
        o = _subln(o0 - lam * o1, g_ref[...], lam_init)
        for r in range(B_GROUP):
            hh = g * B_GROUP + r
            o_ref[:, hh * B_VDIM:(hh + 1) * B_VDIM] = o[r * tq:(r + 1) * tq]


def _diff_prompt(qrot, kvb, lam_qk, g_sub, nb, lp, lam_init):
    m = qrot.shape[0]
    tq = DIFF_Q_TILE
    nq = lp // tq
    return pl.pallas_call(
        functools.partial(_diff_kernel, lam_init=lam_init),
        grid=(nb, nq),
        in_specs=[pl.BlockSpec((tq, B_HEADS * 2 * HEAD_DIM), lambda b, i: (b * nq + i, 1)),
                  pl.BlockSpec((1, lp, B_KV * 2 * HEAD_DIM), lambda b, i: (b, 0, 0)),
                  pl.BlockSpec((1, lp, B_KV * B_VDIM), lambda b, i: (b, 0, 1)),
                  pl.BlockSpec((4, HEAD_DIM), lambda b, i: (0, 0)),
                  pl.BlockSpec((1, B_VDIM), lambda b, i: (0, 0))],
        out_specs=pl.BlockSpec((tq, B_WIDTH), lambda b, i: (b * nq + i, 0)),
        out_shape=jax.ShapeDtypeStruct((m, B_WIDTH), F32),
        compiler_params=_params("arbitrary", "arbitrary"),
        name="diff_prompt",
    )(qrot, kvb, kvb, lam_qk, g_sub)


def _sigmoid(x):
    return 1.0 / (1.0 + jnp.exp(-x))


def _merge_kernel(x_ref, oa_ref, ob_ref, za_ref, zb_ref, ga_ref, gb_ref, wa_ref, wb_ref, wo_ref, g_ref,
                  x_out_ref, n_out_ref):
    za, zb = za_ref[...], zb_ref[...]
    ya = (oa_ref[...] * (za * _sigmoid(za))).astype(BF16)
    yb = (ob_ref[...] * (zb * _sigmoid(zb))).astype(BF16)
    branch_a = jnp.dot(ya, wa_ref[...], preferred_element_type=F32)
    branch_b = jnp.dot(yb, wb_ref[...], preferred_element_type=F32)
    mix = _sigmoid(ga_ref[...]) * branch_a + _sigmoid(gb_ref[...]) * branch_b
    x_new = x_ref[...] + jnp.dot(mix.astype(BF16), wo_ref[...], preferred_element_type=F32)
    x_out_ref[...] = x_new
    ms = jnp.mean(x_new * x_new, axis=-1, keepdims=True)
    n_out_ref[...] = (x_new * lax.rsqrt(ms + EPS) * g_ref[...]).astype(n_out_ref.dtype)


def _merge(x, o_a, o_b, zg, w_pa, w_pb, w_o, g_next, norm_dtype):
    m, d = x.shape
    tm = min(MERGE_TILE, m)
    row = lambda width, col: pl.BlockSpec((tm, width), lambda i: (i, col))
    full = lambda a: pl.BlockSpec(a.shape, lambda i: (0, 0), pipeline_mode=pl.Buffered(1))
    return pl.pallas_call(
        _merge_kernel,
        grid=(m // tm,),
        in_specs=[row(d, 0), row(A_WIDTH, 0), row(B_WIDTH, 0),
                  row(A_WIDTH, 0), row(B_WIDTH, 1), row(d, 1), row(d, 2),
                  full(w_pa), full(w_pb), full(w_o), pl.BlockSpec((1, d), lambda i: (0, 0))],
        out_specs=[row(d, 0), row(d, 0)],
        out_shape=[jax.ShapeDtypeStruct((m, d), F32), jax.ShapeDtypeStruct((m, d), norm_dtype)],
        compiler_params=_params("arbitrary"),
        name="merge",
    )(x, o_a, o_b, zg, zg, zg, zg, w_pa, w_pb, w_o, g_next.reshape(1, d).astype(F32))


def _decode_kernel(pt_ref, qi_ref, w_ref, qa_ref, qb_ref, kin_ref, kan_ref, van_ref, kbn_ref, vbn_ref,
                   lq_ref, g_ref, cki_ref, cka_ref, cva_ref, ckb_ref, cvb_ref,
                   oa_ref, ob_ref, bki, bka, bva, bkb, bvb, sems, *, layer, past_len, n_new, k_top, lam_init):
    b = pl.program_id(0)
    nb = pl.num_programs(0)
    n_pages = bki.shape[1] - 1
    page = bki.shape[3]
    s_all = (n_pages + 1) * page
    pairs = ((cki_ref, bki), (cka_ref, bka), (cva_ref, bva), (ckb_ref, bkb), (cvb_ref, bvb))

    def page_copy(src, dst, which, seq, slot, p):
        return pltpu.make_async_copy(src.at[layer, pt_ref[seq, p]], dst.at[slot, p], sems.at[slot, which])

    def start_all(seq, slot):
        for p in range(n_pages):
            for which, (src, dst) in enumerate(pairs):
                page_copy(src, dst, which, seq, slot, p).start(priority=p % 2)

    @pl.when(b == 0)
    def _():
        start_all(0, 0)

    slot = lax.rem(b, 2)

    @pl.when(b + 1 < nb)
    def _():
        start_all(b + 1, 1 - slot)

    for buf, new in ((bki, kin_ref), (bka, kan_ref), (bva, van_ref), (bkb, kbn_ref), (bvb, vbn_ref)):
        buf[slot, n_pages] = jnp.zeros(buf.shape[2:], F32)
        buf[slot, n_pages, 0:new.shape[1]] = new[0]

    for p in range(n_pages):
        for which, (src, dst) in enumerate(pairs):
            page_copy(src, dst, which, b, slot, p).wait()

    rows = DEC_ROWS
    col = lax.broadcasted_iota(I32, (rows, s_all), 1)
    tok = jnp.minimum(lax.broadcasted_iota(I32, (rows, s_all), 0), n_new - 1)
    causal = (col <= past_len + tok) & (col < past_len + n_new)

    kidx_t = jnp.concatenate([bki[slot, p] for p in range(n_pages + 1)], axis=1).astype(BF16)
    d = jnp.dot(qi_ref[0], kidx_t, preferred_element_type=F32)
    w_idx = w_ref[0] * (IDX_DIM ** -0.5)
    acc = jnp.zeros((rows, s_all), F32)
    for h in range(IDX_HEADS):
        acc = acc + jnp.maximum(d[h * rows:(h + 1) * rows], 0.0) * w_idx[:, h:h + 1]
    score = jnp.where(causal, _one_zero(acc), jnp.nan)

    def count_ge(value):
        return jnp.where(score >= value, 1.0, 0.0)

    scale = HEAD_DIM ** -0.5

    def softmax_pv(s, v):
        m = jnp.max(s, axis=1, keepdims=True)
        p = jnp.exp(s - m)
        l = jnp.sum(p, axis=1, keepdims=True)
        return jnp.dot(p.astype(v.dtype), v, preferred_element_type=F32) / l

    lam = _lambda_full(lq_ref, lam_init)
    causal_bias = jnp.where(causal, 0.0, NEG_BIG)
    bias_b = jnp.concatenate([causal_bias] * B_GROUP, axis=0)
    for g in range(B_KV):
        vg = jnp.concatenate(
            [bvb[slot, :, pl.ds(hf * B_KV + g, page, stride=2 * B_KV), :].reshape(s_all, HEAD_DIM)
             for hf in range(2)], axis=1).astype(BF16)
        outs = []
        for c2 in range(2):
            kgc = bkb[slot, :, pl.ds(g * 2 + c2, page, stride=2 * B_KV), :].reshape(s_all, HEAD_DIM).astype(BF16)
            s = lax.dot_general(qb_ref[0, g, c2], kgc, _NT, preferred_element_type=F32) * scale + bias_b
            outs.append(softmax_pv(s, vg))
        o = _subln(outs[0] - lam * outs[1], g_ref[...], lam_init)
        for r in range(B_GROUP):
            hh = g * B_GROUP + r
            ob_ref[0, :, hh * B_VDIM:(hh + 1) * B_VDIM] = o[r * rows:(r + 1) * rows]

    thr_key, = _search_threshold([lambda key: count_ge(_key_value(key))], rows, float(k_top),
                                 bits=DEC_SEARCH_BITS, unroll=True)
    thr = _threshold_value(thr_key)
    c_ge = jnp.sum(count_ge(thr), axis=1, keepdims=True)
    c_gt = jnp.sum(jnp.where(score > thr, 1.0, 0.0), axis=1, keepdims=True)
    need = float(k_top) - c_gt
    sel = (score >= thr) & (score > -jnp.inf) & (score < jnp.inf)
    sel_bias = jnp.where(sel, 0.0, NEG_BIG)

    def drop_ties(sel_bias):
        eq = score == thr
        tri = _strict_upper(LANES)
        run = jnp.zeros((rows, 1), F32)
        drops = []
        for j in range(s_all // LANES):
            eqj = jnp.where(eq[:, j * LANES:(j + 1) * LANES], 1.0, 0.0)
            before = jnp.dot(eqj.astype(BF16), tri, preferred_element_type=F32) + run
            drops.append(before >= need)
            run = run + jnp.sum(eqj, axis=1, keepdims=True)
        drop = eq & jnp.concatenate(drops, axis=1) & (c_ge > float(k_top))
        return jnp.where(drop, NEG_BIG, sel_bias)

    tie_rows = jnp.where((c_ge > float(k_top)) & (thr_key > jnp.int32(INT_MIN)), 1.0, 0.0)
    sel_bias = lax.cond(jnp.max(tie_rows) > 0.0, drop_ties, lambda x: x, sel_bias)

    bias_a = jnp.concatenate([sel_bias] * A_GROUP, axis=0)
    for g in range(A_KV):
        kg = bka[slot, :, pl.ds(g, page, stride=A_KV), :].reshape(s_all, HEAD_DIM).astype(BF16)
        vg = bva[slot, :, pl.ds(g, page, stride=A_KV), :].reshape(s_all, HEAD_DIM).astype(BF16)
        s = lax.dot_general(qa_ref[0, g], kg, _NT, preferred_element_type=F32) * scale + bias_a
        o = softmax_pv(s, vg)
        for r in range(A_GROUP):
            hh = g * A_GROUP + r
            oa_ref[0, :, hh * HEAD_DIM:(hh + 1) * HEAD_DIM] = o[r * rows:(r + 1) * rows]


def _decode_attend(page_table, qi8, w8, qa8, qb8, new_rows, lam_qk, g_sub, caches, layer, past_len, n_new,
                   k_top, lam_init):
    nseq, n_pages = page_table.shape
    page = caches[0].shape[2]
    seq_spec = lambda a: pl.BlockSpec((1,) + a.shape[1:], lambda b, pt: (b,) + (0,) * (a.ndim - 1))
    const = lambda a: pl.BlockSpec(a.shape, lambda b, pt: (0,) * a.ndim)
    any_spec = pl.BlockSpec(memory_space=pl.ANY)
    per_seq = [qi8, w8, qa8, qb8] + list(new_rows)
    buf = lambda c: pltpu.VMEM((2, n_pages + 1) + c.shape[2:], F32)
    grid_spec = pltpu.PrefetchScalarGridSpec(
        num_scalar_prefetch=1,
        grid=(nseq,),
        in_specs=[seq_spec(a) for a in per_seq] + [const(lam_qk), const(g_sub)] + [any_spec] * 5,
        out_specs=[pl.BlockSpec((1, DEC_ROWS, A_WIDTH), lambda b, pt: (b, 0, 0)),
                   pl.BlockSpec((1, DEC_ROWS, B_WIDTH), lambda b, pt: (b, 0, 0))],
        scratch_shapes=[buf(c) for c in caches] + [pltpu.SemaphoreType.DMA((2, 5))],
    )
    return pl.pallas_call(
        functools.partial(_decode_kernel, layer=layer, past_len=past_len, n_new=n_new, k_top=k_top,
                          lam_init=lam_init),
        grid_spec=grid_spec,
        out_shape=[jax.ShapeDtypeStruct((nseq, DEC_ROWS, A_WIDTH), F32),
                   jax.ShapeDtypeStruct((nseq, DEC_ROWS, B_WIDTH), F32)],
        compiler_params=_params("arbitrary"),
        name="decode_attend",
    )(page_table, *per_seq, lam_qk, g_sub, *caches)


def _rope_tables(pos):
    def cs(dim):
        half = dim // 2
        inv = ROPE_THETA ** (-jnp.arange(half, dtype=F32) / half)
        ang = pos.astype(F32)[:, None] * inv[None, :]
        return jnp.cos(ang), jnp.sin(ang)
    cos, sin = cs(HEAD_DIM)
    c128 = jnp.concatenate([cos, cos], axis=1)
    s128 = jnp.concatenate([-sin, sin], axis=1)
    cos, sin = cs(IDX_DIM)
    zero = jnp.zeros_like(sin)
    c64 = jnp.concatenate([cos, cos, cos, cos], axis=1)
    lo64 = jnp.concatenate([-sin, zero, -sin, zero], axis=1)
    hi64 = jnp.concatenate([zero, sin, zero, sin], axis=1)
    n = pos.shape[0]
    tail = jnp.concatenate([jnp.full((n, IDX_HEADS), IDX_HEADS ** -0.5, F32),
                            jnp.zeros((n, LANES - IDX_DIM - IDX_HEADS), F32)], axis=1)
    ck = jnp.concatenate([cos, cos, tail], axis=1)
    zero_tail = jnp.zeros((n, LANES - IDX_DIM), F32)
    klo = jnp.concatenate([-sin, zero, zero_tail], axis=1)
    khi = jnp.concatenate([zero, sin, zero_tail], axis=1)
    return (c128, s128), (c64, lo64, hi64, ck, klo, khi)


def _split_weights(w):
    sizes = (A_WIDTH, A_KV * HEAD_DIM, A_KV * HEAD_DIM, A_WIDTH, IDX_HEADS * IDX_DIM, IDX_DIM, IDX_HEADS,
             B_HEADS * 2 * HEAD_DIM, B_KV * 2 * HEAD_DIM, B_KV * B_VDIM, B_WIDTH, w.shape[0], w.shape[0])
    pts = np.cumsum(sizes)[:-1]
    q_a, k_a, v_a, z_a, qi, ki, wi, q_b, k_b, v_b, z_b, g_a, g_b = jnp.split(w, pts, axis=1)
    pad = jnp.zeros((w.shape[0], LANES - IDX_DIM - IDX_HEADS), w.dtype)
    cat = lambda parts: jnp.concatenate(parts, axis=1).astype(BF16)
    return (cat([q_a, q_b]), cat([k_b, v_b, k_a, v_a]), cat([qi, ki, wi, pad]), cat([z_a, z_b, g_a, g_b]))


def _project_all(h, weights, tabs, nb, lp, n_keep, expand):
    w_q, w_kv, w_idx, w_zg = weights
    qrot = _proj_q(h, w_q, tabs[0])
    kv = _proj_kv(h.reshape(nb, lp, h.shape[1]), w_kv, tabs[0], n_keep)
    idx = _proj_idx(h, w_idx, tabs[1], expand)
    zg = _proj_plain(h, w_zg, GATE_COL_TILE)
    return qrot, kv, idx, zg


def kernel(x_prompt, x_sample, cache_k_a, cache_v_a, cache_k_idx, cache_k_b, cache_v_b, page_table,
           meta_tokens, norm_g, w_in, lam_qk, subln_g, w_proj_a, w_proj_b, w_out, final_norm_g):
    nb, seq, d = x_prompt.shape
    nseq, n_new, _ = x_sample.shape
    depth = norm_g.shape[0]
    n_meta = meta_tokens.shape[0]
    n_pages = page_table.shape[1]
    page = cache_k_a.shape[2]
    past_len = n_pages * page
    n_tok = n_meta + seq
    lp = -(-n_tok // ROW_TILE) * ROW_TILE
    k_top_p = min(TOP_K_MAX, seq // 4)
    k_top_s = min(TOP_K_MAX, (past_len + n_new) // 4)

    meta = jnp.broadcast_to(meta_tokens.astype(x_prompt.dtype)[None], (nb, n_meta, d))
    xp = jnp.concatenate([meta, x_prompt, jnp.zeros((nb, lp - n_tok, d), x_prompt.dtype)], axis=1)
    xp = xp.reshape(nb * lp, d)
    xs = x_sample.reshape(nseq * n_new, d)
    tabs_p = _rope_tables(jnp.arange(lp, dtype=I32))
    tabs_s = _rope_tables(past_len + jnp.arange(nseq * n_new, dtype=I32) % n_new)

    np_, pg = cache_k_a.shape[1], cache_k_a.shape[2]
    caches = (jnp.swapaxes(cache_k_idx, 2, 3),
              cache_k_a.reshape(depth, np_, pg * A_KV, HEAD_DIM),
              cache_v_a.reshape(depth, np_, pg * A_KV, HEAD_DIM),
              cache_k_b.reshape(depth, np_, pg * B_KV * 2, HEAD_DIM),
              cache_v_b.reshape(depth, np_, pg, B_KV, 2, HEAD_DIM).transpose(0, 1, 2, 4, 3, 5)
              .reshape(depth, np_, pg * B_KV * 2, HEAD_DIM))

    hp = _rmsnorm(xp, norm_g[0], BF16)
    hs = _rmsnorm(xs, norm_g[0], BF16)
    rows_p = [[] for _ in range(5)]
    rows_s = [[] for _ in range(5)]
    pad_t = lambda a, axis: jnp.pad(a, [(0, DEC_ROWS - n_new) if ax == axis else (0, 0) for ax in range(a.ndim)])
    for l in range(depth):
        lam_init = _lambda_init(l)
        weights = _split_weights(w_in[l])
        w_pa, w_pb, w_o = w_proj_a[l].astype(BF16), w_proj_b[l].astype(BF16), w_out[l].astype(BF16)
        last = l + 1 == depth
        g_next = final_norm_g if last else norm_g[l + 1]
        norm_dtype = F32 if last else BF16
        lq = lam_qk[l].astype(F32)
        g_sub = subln_g[l].reshape(1, B_VDIM).astype(F32)

        qrot, (kvb, nkb, nvb, nka, nva), (qe, kw, kk), zg = _project_all(hp, weights, tabs_p, nb, lp, n_tok, True)
        o_a = _dsa_prompt(qrot, qe, kw, kk, kvb, nb, lp, k_top_p)
        o_b = _diff_prompt(qrot, kvb, lq, g_sub, nb, lp, lam_init)
        xp, hp = _merge(xp, o_a, o_b, zg, w_pa, w_pb, w_o, g_next, norm_dtype)
        ki_p = kw.reshape(nb, lp, LANES)[:, :n_tok, :IDX_DIM]
        for i, a in enumerate((nka.reshape(nb, n_tok, A_KV, HEAD_DIM), nva.reshape(nb, n_tok, A_KV, HEAD_DIM), ki_p,
                               nkb.reshape(nb, n_tok, B_KV, 2, HEAD_DIM), nvb.reshape(nb, n_tok, B_KV, B_VDIM))):
            rows_p[i].append(a)

        ms = nseq * n_new
        qrot, (_, nkb, nvb, nka, nva), (qi, kw, _), zg = _project_all(hs, weights, tabs_s, 1, ms, ms, False)
        qa8 = pad_t(qrot[:, :A_WIDTH].reshape(nseq, n_new, A_KV, A_GROUP, HEAD_DIM).transpose(0, 2, 3, 1, 4), 3)
        qa8 = qa8.reshape(nseq, A_KV, A_GROUP * DEC_ROWS, HEAD_DIM)
        qb8 = qrot[:, A_WIDTH:].reshape(nseq, n_new, B_KV, B_GROUP, 2, HEAD_DIM).transpose(0, 2, 4, 3, 1, 5)
        qb8 = pad_t(qb8, 4).reshape(nseq, B_KV, 2, B_GROUP * DEC_ROWS, HEAD_DIM)
        qi8 = pad_t(qi.reshape(nseq, n_new, IDX_HEADS, IDX_DIM).transpose(0, 2, 1, 3), 2)
        qi8 = qi8.reshape(nseq, IDX_HEADS * DEC_ROWS, IDX_DIM)
        w8 = pad_t(kw[:, IDX_DIM:IDX_DIM + IDX_HEADS].reshape(nseq, n_new, IDX_HEADS), 1)
        ki_s = kw[:, :IDX_DIM].reshape(nseq, n_new, IDX_DIM)
        new_s = (nka.reshape(nseq, n_new, A_KV, HEAD_DIM), nva.reshape(nseq, n_new, A_KV, HEAD_DIM), ki_s,
                 nkb.reshape(nseq, n_new, B_KV, 2, HEAD_DIM), nvb.reshape(nseq, n_new, B_KV, B_VDIM))
        new_rows = [pad_t(new_s[j], 1) for j in (2, 0, 1, 3, 4)]
        new_rows[1:4] = [a.reshape(nseq, -1, HEAD_DIM) for a in new_rows[1:4]]
        new_rows[0] = jnp.pad(jnp.swapaxes(ki_s, 1, 2), ((0, 0), (0, 0), (0, pg - n_new)))
        new_rows[4] = (new_rows[4].reshape(nseq, DEC_ROWS, B_KV, 2, HEAD_DIM).transpose(0, 1, 3, 2, 4)
                       .reshape(nseq, -1, HEAD_DIM))
        o_a8, o_b8 = _decode_attend(page_table, qi8, w8, qa8, qb8, new_rows, lq, g_sub, caches, l, past_len,
                                    n_new, k_top_s, lam_init)
        o_a = o_a8[:, :n_new].reshape(ms, A_WIDTH)
        o_b = o_b8[:, :n_new].reshape(ms, B_WIDTH)
        xs, hs = _merge(xs, o_a, o_b, zg, w_pa, w_pb, w_o, g_next, norm_dtype)
        for i, a in enumerate(new_s):
            rows_s[i].append(a)

    y_prompt = hp.reshape(nb, lp, d)[:, n_meta:n_tok]
    y_sample = hs.reshape(nseq, n_new, d)
    return (y_prompt, y_sample) + tuple(jnp.stack(r) for r in rows_p) + tuple(jnp.stack(r) for r in rows_s)
```

```python
import functools
import math

import jax
import jax.numpy as jnp
import numpy as np
from jax import lax
from jax.experimental import pallas as pl
from jax.experimental.pallas import tpu as pltpu

HEAD_DIM = 128
A_HEADS = 8
A_KV = 2
A_GROUP = A_HEADS // A_KV
A_WIDTH = A_HEADS * HEAD_DIM
IDX_HEADS = 16
IDX_DIM = 64
B_HEADS = 4
B_KV = 2
B_GROUP = B_HEADS // B_KV
B_VDIM = 2 * HEAD_DIM
B_WIDTH = B_HEADS * B_VDIM
TOP_K_MAX = 256
ROPE_THETA = 10000.0
EPS = 1e-6

LANES = 128
ROW_TILE = 256
GATE_ROW_TILE = 512
GATE_COL_TILE = 2048
KEY_CHUNK = 256
ATTN_STEP_WIDTHS = (4, 2, 1)
DSA_Q_TILE = 256
SEARCH_ROWS = 128
SCORE_ROWS = 64
DIFF_Q_TILE = 256
MERGE_TILE = 256
DEC_ROWS = 8
DEC_SEARCH_BITS = 3
VMEM_LIMIT = 48 * 1024 * 1024

LOG2_E = 1.4426950408889634
NEG_BIG = -1e30
INT_MIN = -2 ** 31
KEY_NEG_INF = -2139095041

F32 = jnp.float32
BF16 = jnp.bfloat16
I32 = jnp.int32

_NT = (((1,), (1,)), ((), ()))


def _params(*sem):
    return pltpu.CompilerParams(dimension_semantics=sem, vmem_limit_bytes=VMEM_LIMIT)


def _lambda_init(layer):
    return 0.8 - 0.6 * math.exp(-0.3 * layer)


def _rmsnorm_kernel(x_ref, g_ref, o_ref):
    x = x_ref[...]
    ms = jnp.mean(x * x, axis=-1, keepdims=True)
    o_ref[...] = (x * lax.rsqrt(ms + EPS) * g_ref[...]).astype(o_ref.dtype)


def _rmsnorm(x, g, out_dtype):
    m, d = x.shape
    tm = min(ROW_TILE, m)
    return pl.pallas_call(
        _rmsnorm_kernel,
        grid=(m // tm,),
        in_specs=[pl.BlockSpec((tm, d), lambda i: (i, 0)),
                  pl.BlockSpec((1, d), lambda i: (0, 0))],
        out_specs=pl.BlockSpec((tm, d), lambda i: (i, 0)),
        out_shape=jax.ShapeDtypeStruct((m, d), out_dtype),
        compiler_params=_params("arbitrary"),
        name="rmsnorm",
    )(x, g.reshape(1, d).astype(F32))


def _rope128(a, c, s):
    return a * c + pltpu.roll(a, HEAD_DIM // 2, 1) * s


def _rope64(a, c, s_lo, s_hi):
    return a * c + pltpu.roll(a, LANES - IDX_DIM // 2, 1) * s_lo + pltpu.roll(a, IDX_DIM // 2, 1) * s_hi


def _proj_q_kernel(h_ref, w_ref, c_ref, s_ref, o_ref):
    acc = jnp.dot(h_ref[...], w_ref[...], preferred_element_type=F32)
    c, s = c_ref[...], s_ref[...]
    for j in range(acc.shape[1] // LANES):
        sl = slice(j * LANES, (j + 1) * LANES)
        o_ref[:, sl] = _rope128(acc[:, sl], c, s).astype(o_ref.dtype)


def _proj_kv_kernel(h_ref, w_ref, c_ref, s_ref, kv_ref, kb_ref, vb_ref, ka_ref, va_ref):
    acc = jnp.dot(h_ref[0], w_ref[...], preferred_element_type=F32)
    c, s = c_ref[...], s_ref[...]
    outs = ((kb_ref, 0, 4, True), (vb_ref, 4, 4, False), (ka_ref, 8, 2, True), (va_ref, 10, 2, False))
    for ref, start, count, rot in outs:
        for j in range(count):
            a = acc[:, (start + j) * LANES:(start + j + 1) * LANES]
            if rot:
                a = _rope128(a, c, s)
            ref[0, :, j * LANES:(j + 1) * LANES] = a
            kv_ref[0, :, (start + j) * LANES:(start + j + 1) * LANES] = a.astype(kv_ref.dtype)


def _proj_idx_kernel(h_ref, w_ref, c_ref, slo_ref, shi_ref, ck_ref, klo_ref, khi_ref,
                     q_ref, kw_ref, kk_ref, *, expand):
    acc = jnp.dot(h_ref[...], w_ref[...], preferred_element_type=F32)
    c, slo, shi = c_ref[...], slo_ref[...], shi_ref[...]
    lane = lax.broadcasted_iota(I32, (acc.shape[0], LANES), 1)
    low = lane < IDX_DIM
    for j in range(IDX_HEADS // 2):
        a = _rope64(acc[:, j * LANES:(j + 1) * LANES], c, slo, shi)
        if expand:
            q_ref[:, (2 * j) * LANES:(2 * j + 1) * LANES] = jnp.where(low, a, 0.0).astype(q_ref.dtype)
            q_ref[:, (2 * j + 1) * LANES:(2 * j + 2) * LANES] = jnp.where(low, 0.0, a).astype(q_ref.dtype)
        else:
            q_ref[:, j * LANES:(j + 1) * LANES] = a.astype(q_ref.dtype)
    kw = _rope64(acc[:, IDX_HEADS * IDX_DIM:], ck_ref[...], klo_ref[...], khi_ref[...])
    kw_ref[...] = kw
    kk_ref[...] = jnp.where(low, kw, pltpu.roll(kw, IDX_DIM, 1)).astype(kk_ref.dtype)


def _proj_plain_kernel(h_ref, w_ref, o_ref):
    o_ref[...] = jnp.dot(h_ref[...], w_ref[...], preferred_element_type=F32).astype(o_ref.dtype)


def _row_spec(tm, width, ntab=None):
    if ntab is None:
        return pl.BlockSpec((tm, width), lambda j, i: (i, 0))
    return pl.BlockSpec((tm, width), lambda j, i: (i % ntab, 0))


def _proj_q(h, w, tabs):
    m, d = h.shape
    n = w.shape[1]
    tm = min(ROW_TILE, m)
    ntab = tabs[0].shape[0] // tm
    return pl.pallas_call(
        _proj_q_kernel,
        grid=(1, m // tm),
        in_specs=[_row_spec(tm, d), pl.BlockSpec((d, n), lambda j, i: (0, 0)),
                  _row_spec(tm, LANES, ntab), _row_spec(tm, LANES, ntab)],
        out_specs=_row_spec(tm, n),
        out_shape=jax.ShapeDtypeStruct((m, n), BF16),
        compiler_params=_params("arbitrary", "arbitrary"),
        name="proj_q",
    )(h, w, tabs[0], tabs[1])


def _proj_kv(h3, w, tabs, n_keep):
    nb, lp, d = h3.shape
    n = w.shape[1]
    tm = min(DIFF_Q_TILE, lp)
    row = lambda width: pl.BlockSpec((1, tm, width), lambda b, i: (b, i, 0))
    tab = pl.BlockSpec((tm, LANES), lambda b, i: (i, 0))
    f32_out = lambda width: jax.ShapeDtypeStruct((nb, n_keep, width), F32)
    return pl.pallas_call(
        _proj_kv_kernel,
        grid=(nb, lp // tm),
        in_specs=[row(d), pl.BlockSpec((d, n), lambda b, i: (0, 0)), tab, tab],
        out_specs=[row(n), row(4 * LANES), row(4 * LANES), row(2 * LANES), row(2 * LANES)],
        out_shape=[jax.ShapeDtypeStruct((nb, lp, n), BF16),
                   f32_out(4 * LANES), f32_out(4 * LANES), f32_out(2 * LANES), f32_out(2 * LANES)],
        compiler_params=_params("arbitrary", "arbitrary"),
        name="proj_kv",
    )(h3, w, tabs[0], tabs[1])


def _proj_idx(h, w, tabs, expand):
    m, d = h.shape
    n = w.shape[1]
    tm = min(ROW_TILE, m)
    ntab = tabs[0].shape[0] // tm
    qw = IDX_HEADS * (LANES if expand else IDX_DIM)
    return pl.pallas_call(
        functools.partial(_proj_idx_kernel, expand=expand),
        grid=(1, m // tm),
        in_specs=[_row_spec(tm, d), pl.BlockSpec((d, n), lambda j, i: (0, 0))]
                 + [_row_spec(tm, LANES, ntab)] * 6,
        out_specs=[_row_spec(tm, qw), _row_spec(tm, LANES), _row_spec(tm, LANES)],
        out_shape=[jax.ShapeDtypeStruct((m, qw), BF16),
                   jax.ShapeDtypeStruct((m, LANES), F32),
                   jax.ShapeDtypeStruct((m, LANES), BF16)],
        compiler_params=_params("arbitrary", "arbitrary"),
        name="proj_idx",
    )(h, w, *tabs)


def _proj_plain(h, w, tn):
    m, d = h.shape
    n = w.shape[1]
    tm = GATE_ROW_TILE if m % GATE_ROW_TILE == 0 else min(ROW_TILE, m)
    return pl.pallas_call(
        _proj_plain_kernel,
        grid=(n // tn, m // tm),
        in_specs=[_row_spec(tm, d), pl.BlockSpec((d, tn), lambda j, i: (0, j))],
        out_specs=pl.BlockSpec((tm, tn), lambda j, i: (i, j)),
        out_shape=jax.ShapeDtypeStruct((m, n), F32),
        compiler_params=_params("arbitrary", "arbitrary"),
        name="proj_gate",
    )(h, w)


def _key_value(key):
    return lax.bitcast_convert_type(key ^ ((key >> 31) & jnp.int32(0x7FFFFFFF)), F32)


def _one_zero(x):
    return jnp.where(x == 0.0, 0.0, x)


def _threshold_value(thr_key):
    return jnp.where(thr_key < jnp.int32(KEY_NEG_INF), -jnp.inf, _key_value(thr_key))


def _search_threshold(count_fns, rows, k_top, bits=1, unroll=False):
    zero = jnp.zeros((rows, 1), I32)
    total = lambda partial: jnp.sum(partial, axis=1, keepdims=True)
    partials = [fn(zero) for fn in count_fns]
    ts = tuple(jnp.where(total(p) >= k_top, zero, jnp.int32(INT_MIN)) for p in partials)
    n_bits = 31

    def refine(shift, nbits, ts):
        cands = [[t | jnp.left_shift(jnp.int32(j), shift) for j in range(1, 2 ** nbits)] for t in ts]
        partials = [[fn(c) for c in cs] for fn, cs in zip(count_fns, cands)]
        out = []
        for t, ps in zip(ts, partials):
            group = jnp.zeros((rows, 1), I32)
            for p in ps:
                group = group + jnp.where(total(p) >= k_top, 1, 0)
            out.append(t | jnp.left_shift(group, shift))
        return tuple(out)

    n_groups, rest = divmod(n_bits, bits)
    if unroll:
        for it in range(n_groups):
            ts = refine(n_bits - bits * (it + 1), bits, ts)
    else:
        ts = lax.fori_loop(0, n_groups, lambda it, ts: refine(n_bits - bits * (it + 1), bits, ts), ts)
    return list(refine(0, rest, ts) if rest else ts)


def _strict_upper(n):
    r = lax.broadcasted_iota(I32, (n, n), 0)
    c = lax.broadcasted_iota(I32, (n, n), 1)
    return jnp.where(r < c, 1.0, 0.0).astype(BF16)


def _dsa_kernel(qa_ref, qe_ref, kw_ref, kk_ref, ka_ref, va_ref, o_ref, sc_ref, *, k_top):
    tq = qa_ref.shape[0]
    kc = KEY_CHUNK
    hr = SEARCH_ROWS
    sr = SCORE_ROWS
    i = pl.program_id(1)
    q0 = i * tq
    nkc = (q0 + tq + kc - 1) // kc
    col_iota = lax.broadcasted_iota(I32, (sr, kc), 1)
    row_iota = lax.broadcasted_iota(I32, (sr, kc), 0)

    slabs = list(range(0, tq, sr))

    def head_dots(c, r0):
        q_stack = jnp.concatenate([qe_ref[r0:r0 + sr, h * LANES:(h + 1) * LANES] for h in range(IDX_HEADS)],
                                  axis=0)
        k0 = pl.multiple_of(c * kc, kc)
        return lax.dot_general(q_stack, kk_ref[pl.ds(k0, kc), :], _NT, preferred_element_type=F32)

    w_cols = {}
    for r0 in slabs:
        w_idx = kw_ref[r0:r0 + sr, IDX_DIM:IDX_DIM + IDX_HEADS] * (IDX_DIM ** -0.5)
        w_cols[r0] = [jnp.broadcast_to(w_idx[:, h:h + 1], (sr, LANES)) for h in range(IDX_HEADS)]

    def slab_keys(d, r0):
        parts = []
        for j in range(kc // LANES):
            acc = jnp.zeros((sr, LANES), F32)
            for h in range(IDX_HEADS):
                acc = acc + jnp.maximum(d[h * sr:(h + 1) * sr, j * LANES:(j + 1) * LANES], 0.0) * w_cols[r0][h]
            parts.append(acc)
        return _one_zero(jnp.concatenate(parts, axis=1))

    def p1_body(c, carry):
        for r0 in slabs:
            sc_ref[c, r0:r0 + sr, :] = slab_keys(head_dots(c, r0), r0)
        return carry
    lax.fori_loop(0, nkc - 1, p1_body, 0)
    for r0 in slabs:
        causal = (nkc - 1) * kc + col_iota <= q0 + r0 + row_iota
        sc_ref[nkc - 1, r0:r0 + sr, :] = jnp.where(causal, slab_keys(head_dots(nkc - 1, r0), r0), jnp.nan)

    search_slabs = list(range(0, tq, hr))

    def count_cmp(cand, strict, r0):
        cand_b = jnp.broadcast_to(cand, (hr, LANES))
        def body(c, acc):
            for j in range(kc // LANES):
                sj = sc_ref[c, r0:r0 + hr, j * LANES:(j + 1) * LANES]
                hit = (sj > cand_b) if strict else (sj >= cand_b)
                acc = acc + jnp.where(hit, 1.0, 0.0)
            return acc
        return lax.fori_loop(0, nkc, body, jnp.zeros((hr, LANES), F32))

    thr_keys = _search_threshold(
        [functools.partial(lambda key, r0: count_cmp(_key_value(key), False, r0), r0=r0) for r0 in search_slabs],
        hr, float(k_top))
    thr_slabs = [_threshold_value(k) for k in thr_keys]
    p_ges = [count_cmp(thr, False, r0) for thr, r0 in zip(thr_slabs, search_slabs)]
    p_gts = [count_cmp(thr, True, r0) for thr, r0 in zip(thr_slabs, search_slabs)]
    for r0, thr_key, thr, p_ge, p_gt in zip(search_slabs, thr_keys, thr_slabs, p_ges, p_gts):
        c_ge = jnp.sum(p_ge, axis=1, keepdims=True)
        need = float(k_top) - jnp.sum(p_gt, axis=1, keepdims=True)
        tie_rows = jnp.where((c_ge > float(k_top)) & (thr_key > jnp.int32(INT_MIN)), 1.0, 0.0)

        @pl.when(jnp.max(tie_rows) > 0.0)
        def _(r0=r0, thr=thr, need=need):
            tri = _strict_upper(kc)
            def body(c, run):
                s = sc_ref[c, r0:r0 + hr, :]
                eq = s == thr
                eqf = jnp.where(eq, 1.0, 0.0)
                before = jnp.dot(eqf.astype(BF16), tri, preferred_element_type=F32) + run
                sc_ref[c, r0:r0 + hr, :] = jnp.where(eq & (before >= need), jnp.nan, s)
                return run + jnp.sum(eqf, axis=1, keepdims=True)
            lax.fori_loop(0, nkc, body, jnp.zeros((hr, 1), F32))

    thr = jnp.concatenate(thr_slabs, axis=0)

    scale2 = HEAD_DIM ** -0.5 * LOG2_E

    def attend(c0, width, carry):
        k0 = pl.multiple_of(c0 * kc, kc)
        score = jnp.concatenate([sc_ref[c0 + u] for u in range(width)], axis=1)
        sel = (score >= thr) & (score > -jnp.inf) & (score < jnp.inf)
        bias = jnp.where(sel, 0.0, NEG_BIG)
        out = []
        for g in range(A_KV):
            qg = jnp.concatenate([qa_ref[:, (g * A_GROUP + r) * HEAD_DIM:(g * A_GROUP + r + 1) * HEAD_DIM]
                                  for r in range(A_GROUP)], axis=0)
            kch = ka_ref[0, pl.ds(k0, width * kc), g * HEAD_DIM:(g + 1) * HEAD_DIM]
            vch = va_ref[0, pl.ds(k0, width * kc), g * HEAD_DIM:(g + 1) * HEAD_DIM]
            s = lax.dot_general(qg, kch, _NT, preferred_element_type=F32)
            ps, stats = [], []
            for r in range(A_GROUP):
                m, l, _ = carry[g * A_GROUP + r]
                sr = s[r * tq:(r + 1) * tq] * scale2 + bias
                m_new = jnp.maximum(m, jnp.max(sr, axis=1, keepdims=True))
                p = jnp.exp2(sr - m_new)
                alpha = jnp.exp2(m - m_new)
                stats.append((m_new, alpha * l + jnp.sum(p, axis=1, keepdims=True), alpha))
                ps.append(p.astype(BF16))
            pv = jnp.dot(jnp.concatenate(ps, axis=0), vch, preferred_element_type=F32)
            for r in range(A_GROUP):
                m_new, l_new, alpha = stats[r]
                out.append((m_new, l_new, alpha * carry[g * A_GROUP + r][2] + pv[r * tq:(r + 1) * tq]))
        return tuple(out)

    init = tuple((jnp.full((tq, 1), NEG_BIG, F32), jnp.zeros((tq, 1), F32),
                  jnp.zeros((tq, HEAD_DIM), F32)) for _ in range(A_HEADS))
    fin, done = init, 0
    for width in ATTN_STEP_WIDTHS:
        n_steps = (nkc - done) // width
        fin = lax.fori_loop(0, n_steps, lambda j, carry, done=done, width=width: attend(done + width * j, width, carry),
                            fin)
        done = done + n_steps * width
    for hh in range(A_HEADS):
        o_ref[:, hh * HEAD_DIM:(hh + 1) * HEAD_DIM] = fin[hh][2] / fin[hh][1]


def _dsa_prompt(qrot, qe, kw, kk, kvb, nb, lp, k_top):
    m = qrot.shape[0]
    tq = DSA_Q_TILE
    nq = lp // tq
    qrow = lambda width: pl.BlockSpec((tq, width), lambda b, i: (b * nq + i, 0))
    return pl.pallas_call(
        functools.partial(_dsa_kernel, k_top=k_top),
        grid=(nb, nq),
        in_specs=[qrow(A_WIDTH), qrow(IDX_HEADS * LANES), qrow(LANES),
                  pl.BlockSpec((lp, LANES), lambda b, i: (b, 0)),
                  pl.BlockSpec((1, lp, A_KV * HEAD_DIM), lambda b, i: (b, 0, 4)),
                  pl.BlockSpec((1, lp, A_KV * HEAD_DIM), lambda b, i: (b, 0, 5))],
        out_specs=qrow(A_WIDTH),
        out_shape=jax.ShapeDtypeStruct((m, A_WIDTH), F32),
        scratch_shapes=[pltpu.VMEM((lp // KEY_CHUNK, tq, KEY_CHUNK), F32)],
        compiler_params=_params("arbitrary", "arbitrary"),
        name="dsa_prompt",
    )(qrot, qe, kw, kk, kvb, kvb)


def _lambda_full(lq_ref, lam_init):
    lq = lq_ref[...]
    a = jnp.sum(lq[0:1] * lq[1:2], axis=1, keepdims=True)
    b = jnp.sum(lq[2:3] * lq[3:4], axis=1, keepdims=True)
    return jnp.exp(a) - jnp.exp(b) + lam_init


def _subln(o, g, lam_init):
    ms = jnp.mean(o * o, axis=-1, keepdims=True)
    return (o * lax.rsqrt(ms + EPS) * g) * (1.0 - lam_init)


def _diff_kernel(qb_ref, kb_ref, vb_ref, lq_ref, g_ref, o_ref, *, lam_init):
    tq = qb_ref.shape[0]
    kc = KEY_CHUNK
    i = pl.program_id(1)
    q0 = i * tq
    n_full = q0 // kc
    n_all = (q0 + tq + kc - 1) // kc
    lam = _lambda_full(lq_ref, lam_init)
    scale2 = HEAD_DIM ** -0.5 * LOG2_E
    rows = B_GROUP * tq
    row_pos = q0 + lax.broadcasted_iota(I32, (tq, kc), 0)
    col_iota = lax.broadcasted_iota(I32, (tq, kc), 1)
    row_pos = jnp.concatenate([row_pos] * B_GROUP, axis=0)
    col_iota = jnp.concatenate([col_iota] * B_GROUP, axis=0)

    combos = [(g, c2) for g in range(B_KV) for c2 in range(2)]

    def step(c0, carry, width, masked):
        k0 = pl.multiple_of(c0 * kc, kc)
        out = []
        for idx, (g, c2) in enumerate(combos):
            m, l, acc = carry[idx]
            qgc = jnp.concatenate(
                [qb_ref[:, ((g * B_GROUP + r) * 2 + c2) * HEAD_DIM:((g * B_GROUP + r) * 2 + c2 + 1) * HEAD_DIM]
                 for r in range(B_GROUP)], axis=0)
            kcol = (g * 2 + c2) * HEAD_DIM
            kch = kb_ref[0, pl.ds(k0, width * kc), kcol:kcol + HEAD_DIM]
            vch = vb_ref[0, pl.ds(k0, width * kc), g * B_VDIM:(g + 1) * B_VDIM]
            s = lax.dot_general(qgc, kch, _NT, preferred_element_type=F32) * scale2
            if masked:
                s = jnp.where(k0 + col_iota <= row_pos, s, NEG_BIG)
            m_new = jnp.maximum(m, jnp.max(s, axis=1, keepdims=True))
            p = jnp.exp2(s - m_new)
            alpha = jnp.exp2(m - m_new)
            l = alpha * l + jnp.sum(p, axis=1, keepdims=True)
            acc = alpha * acc + jnp.dot(p.astype(BF16), vch, preferred_element_type=F32)
            out.append((m_new, l, acc))
        return tuple(out)

    init = tuple((jnp.full((rows, 1), NEG_BIG, F32), jnp.zeros((rows, 1), F32),
                  jnp.zeros((rows, B_VDIM), F32)) for _ in combos)
    carry, done = init, 0
    for width in ATTN_STEP_WIDTHS:
        n_steps = (n_full - done) // width
        carry = lax.fori_loop(0, n_steps, lambda j, cr, done=done, width=width: step(done + width * j, cr, width, False),
                              carry)
        done = done + n_steps * width
    carry = lax.fori_loop(n_full, n_all, lambda c, cr: step(c, cr, 1, True), carry)
    for g in range(B_KV):
        o0, o1 = (carry[g * 2 + c2][2] / carry[g * 2 + c2][1] for c2 in range(2))
        o = _subln(o0 - lam * o1, g_ref[...], lam_init)
        for r in range(B_GROUP):
            hh = g * B_GROUP + r
            o_ref[:, hh * B_VDIM:(hh + 1) * B_VDIM] = o[r * tq:(r + 1) * tq]


def _diff_prompt(qrot, kvb, lam_qk, g_sub, nb, lp, lam_init):
    m = qrot.shape[0]
    tq = DIFF_Q_TILE
    nq = lp // tq
    return pl.pallas_call(
        functools.partial(_diff_kernel, lam_init=lam_init),
        grid=(nb, nq),
        in_specs=[pl.BlockSpec((tq, B_HEADS * 2 * HEAD_DIM), lambda b, i: (b * nq + i, 1)),
                  pl.BlockSpec((1, lp, B_KV * 2 * HEAD_DIM), lambda b, i: (b, 0, 0)),
                  pl.BlockSpec((1, lp, B_KV * B_VDIM), lambda b, i: (b, 0, 1)),
                  pl.BlockSpec((4, HEAD_DIM), lambda b, i: (0, 0)),
                  pl.BlockSpec((1, B_VDIM), lambda b, i: (0, 0))],
        out_specs=pl.BlockSpec((tq, B_WIDTH), lambda b, i: (b * nq + i, 0)),
        out_shape=jax.ShapeDtypeStruct((m, B_WIDTH), F32),
        compiler_params=_params("arbitrary", "arbitrary"),
        name="diff_prompt",
    )(qrot, kvb, kvb, lam_qk, g_sub)


def _sigmoid(x):
    return 1.0 / (1.0 + jnp.exp(-x))


def _merge_kernel(x_ref, oa_ref, ob_ref, za_ref, zb_ref, ga_ref, gb_ref, wa_ref, wb_ref, wo_ref, g_ref,
                  x_out_ref, n_out_ref):
    za, zb = za_ref[...], zb_ref[...]
    ya = (oa_ref[...] * (za * _sigmoid(za))).astype(BF16)
    yb = (ob_ref[...] * (zb * _sigmoid(zb))).astype(BF16)
    branch_a = jnp.dot(ya, wa_ref[...], preferred_element_type=F32)
    branch_b = jnp.dot(yb, wb_ref[...], preferred_element_type=F32)
    mix = _sigmoid(ga_ref[...]) * branch_a + _sigmoid(gb_ref[...]) * branch_b
    x_new = x_ref[...] + jnp.dot(mix.astype(BF16), wo_ref[...], preferred_element_type=F32)
    x_out_ref[...] = x_new
    ms = jnp.mean(x_new * x_new, axis=-1, keepdims=True)
    n_out_ref[...] = (x_new * lax.rsqrt(ms + EPS) * g_ref[...]).astype(n_out_ref.dtype)


def _merge(x, o_a, o_b, zg, w_pa, w_pb, w_o, g_next, norm_dtype):
    m, d = x.shape
    tm = min(MERGE_TILE, m)
    row = lambda width, col: pl.BlockSpec((tm, width), lambda i: (i, col))
    full = lambda a: pl.BlockSpec(a.shape, lambda i: (0, 0), pipeline_mode=pl.Buffered(1))
    return pl.pallas_call(
        _merge_kernel,
        grid=(m // tm,),
        in_specs=[row(d, 0), row(A_WIDTH, 0), row(B_WIDTH, 0),
                  row(A_WIDTH, 0), row(B_WIDTH, 1), row(d, 1), row(d, 2),
                  full(w_pa), full(w_pb), full(w_o), pl.BlockSpec((1, d), lambda i: (0, 0))],
        out_specs=[row(d, 0), row(d, 0)],
        out_shape=[jax.ShapeDtypeStruct((m, d), F32), jax.ShapeDtypeStruct((m, d), norm_dtype)],
        compiler_params=_params("arbitrary"),
        name="merge",
    )(x, o_a, o_b, zg, zg, zg, zg, w_pa, w_pb, w_o, g_next.reshape(1, d).astype(F32))


def _decode_kernel(pt_ref, qi_ref, w_ref, qa_ref, qb_ref, kin_ref, kan_ref, van_ref, kbn_ref, vbn_ref,
                   lq_ref, g_ref, cki_ref, cka_ref, cva_ref, ckb_ref, cvb_ref,
                   oa_ref, ob_ref, bki, bka, bva, bkb, bvb, sems, *, layer, past_len, n_new, k_top, lam_init):
    b = pl.program_id(0)
    nb = pl.num_programs(0)
    n_pages = bki.shape[1] - 1
    page = bki.shape[3]
    s_all = (n_pages + 1) * page
    pairs = ((cki_ref, bki), (cka_ref, bka), (cva_ref, bva), (ckb_ref, bkb), (cvb_ref, bvb))

    def page_copy(src, dst, which, seq, slot, p):
        return pltpu.make_async_copy(src.at[layer, pt_ref[seq, p]], dst.at[slot, p], sems.at[slot, which])

    def start_all(seq, slot):
        for p in range(n_pages):
            for which, (src, dst) in enumerate(pairs):
                page_copy(src, dst, which, seq, slot, p).start(priority=p % 2)

    @pl.when(b == 0)
    def _():
        start_all(0, 0)

    slot = lax.rem(b, 2)

    @pl.when(b + 1 < nb)
    def _():
        start_all(b + 1, 1 - slot)

    for buf, new in ((bki, kin_ref), (bka, kan_ref), (bva, van_ref), (bkb, kbn_ref), (bvb, vbn_ref)):
        buf[slot, n_pages] = jnp.zeros(buf.shape[2:], F32)
        buf[slot, n_pages, 0:new.shape[1]] = new[0]

    for p in range(n_pages):
        for which, (src, dst) in enumerate(pairs):
            page_copy(src, dst, which, b, slot, p).wait()

    rows = DEC_ROWS
    col = lax.broadcasted_iota(I32, (rows, s_all), 1)
    tok = jnp.minimum(lax.broadcasted_iota(I32, (rows, s_all), 0), n_new - 1)
    causal = (col <= past_len + tok) & (col < past_len + n_new)

    kidx_t = jnp.concatenate([bki[slot, p] for p in range(n_pages + 1)], axis=1).astype(BF16)
    d = jnp.dot(qi_ref[0], kidx_t, preferred_element_type=F32)
    w_idx = w_ref[0] * (IDX_DIM ** -0.5)
    acc = jnp.zeros((rows, s_all), F32)
    for h in range(IDX_HEADS):
        acc = acc + jnp.maximum(d[h * rows:(h + 1) * rows], 0.0) * w_idx[:, h:h + 1]
    score = jnp.where(causal, _one_zero(acc), jnp.nan)

    def count_ge(value):
        return jnp.where(score >= value, 1.0, 0.0)

    scale = HEAD_DIM ** -0.5

    def softmax_pv(s, v):
        m = jnp.max(s, axis=1, keepdims=True)
        p = jnp.exp(s - m)
        l = jnp.sum(p, axis=1, keepdims=True)
        return jnp.dot(p.astype(v.dtype), v, preferred_element_type=F32) / l

    lam = _lambda_full(lq_ref, lam_init)
    causal_bias = jnp.where(causal, 0.0, NEG_BIG)
    bias_b = jnp.concatenate([causal_bias] * B_GROUP, axis=0)
    for g in range(B_KV):
        vg = jnp.concatenate(
            [bvb[slot, :, pl.ds(hf * B_KV + g, page, stride=2 * B_KV), :].reshape(s_all, HEAD_DIM)
             for hf in range(2)], axis=1).astype(BF16)
        outs = []
        for c2 in range(2):
            kgc = bkb[slot, :, pl.ds(g * 2 + c2, page, stride=2 * B_KV), :].reshape(s_all, HEAD_DIM).astype(BF16)
            s = lax.dot_general(qb_ref[0, g, c2], kgc, _NT, preferred_element_type=F32) * scale + bias_b
            outs.append(softmax_pv(s, vg))
        o = _subln(outs[0] - lam * outs[1], g_ref[...], lam_init)
        for r in range(B_GROUP):
            hh = g * B_GROUP + r
            ob_ref[0, :, hh * B_VDIM:(hh + 1) * B_VDIM] = o[r * rows:(r + 1) * rows]

    raw_a = [lax.dot_general(qa_ref[0, g],
                             bka[slot, :, pl.ds(g, page, stride=A_KV), :].reshape(s_all, HEAD_DIM).astype(BF16),
                             _NT, preferred_element_type=F32) * scale for g in range(A_KV)]

    thr_key, = _search_threshold([lambda key: count_ge(_key_value(key))], rows, float(k_top),
                                 bits=DEC_SEARCH_BITS, unroll=True)
    thr = _threshold_value(thr_key)
    c_ge = jnp.sum(count_ge(thr), axis=1, keepdims=True)
    c_gt = jnp.sum(jnp.where(score > thr, 1.0, 0.0), axis=1, keepdims=True)
    need = float(k_top) - c_gt
    sel = (score >= thr) & (score > -jnp.inf) & (score < jnp.inf)
    sel_bias = jnp.where(sel, 0.0, NEG_BIG)

    def drop_ties(sel_bias):
        eq = score == thr
        tri = _strict_upper(LANES)
        run = jnp.zeros((rows, 1), F32)
        drops = []
        for j in range(s_all // LANES):
            eqj = jnp.where(eq[:, j * LANES:(j + 1) * LANES], 1.0, 0.0)
            before = jnp.dot(eqj.astype(BF16), tri, preferred_element_type=F32) + run
            drops.append(before >= need)
            run = run + jnp.sum(eqj, axis=1, keepdims=True)
        drop = eq & jnp.concatenate(drops, axis=1) & (c_ge > float(k_top))
        return jnp.where(drop, NEG_BIG, sel_bias)

    tie_rows = jnp.where((c_ge > float(k_top)) & (thr_key > jnp.int32(INT_MIN)), 1.0, 0.0)
    sel_bias = lax.cond(jnp.max(tie_rows) > 0.0, drop_ties, lambda x: x, sel_bias)

    bias_a = jnp.concatenate([sel_bias] * A_GROUP, axis=0)
    for g in range(A_KV):
        vg = bva[slot, :, pl.ds(g, page, stride=A_KV), :].reshape(s_all, HEAD_DIM).astype(BF16)
        o = softmax_pv(raw_a[g] + bias_a, vg)
        for r in range(A_GROUP):
            hh = g * A_GROUP + r
            oa_ref[0, :, hh * HEAD_DIM:(hh + 1) * HEAD_DIM] = o[r * rows:(r + 1) * rows]


def _decode_attend(page_table, qi8, w8, qa8, qb8, new_rows, lam_qk, g_sub, caches, layer, past_len, n_new,
                   k_top, lam_init):
    nseq, n_pages = page_table.shape
    page = caches[0].shape[2]
    seq_spec = lambda a: pl.BlockSpec((1,) + a.shape[1:], lambda b, pt: (b,) + (0,) * (a.ndim - 1))
    const = lambda a: pl.BlockSpec(a.shape, lambda b, pt: (0,) * a.ndim)
    any_spec = pl.BlockSpec(memory_space=pl.ANY)
    per_seq = [qi8, w8, qa8, qb8] + list(new_rows)
    buf = lambda c: pltpu.VMEM((2, n_pages + 1) + c.shape[2:], F32)
    grid_spec = pltpu.PrefetchScalarGridSpec(
        num_scalar_prefetch=1,
        grid=(nseq,),
        in_specs=[seq_spec(a) for a in per_seq] + [const(lam_qk), const(g_sub)] + [any_spec] * 5,
        out_specs=[pl.BlockSpec((1, DEC_ROWS, A_WIDTH), lambda b, pt: (b, 0, 0)),
                   pl.BlockSpec((1, DEC_ROWS, B_WIDTH), lambda b, pt: (b, 0, 0))],
        scratch_shapes=[buf(c) for c in caches] + [pltpu.SemaphoreType.DMA((2, 5))],
    )
    return pl.pallas_call(
        functools.partial(_decode_kernel, layer=layer, past_len=past_len, n_new=n_new, k_top=k_top,
                          lam_init=lam_init),
        grid_spec=grid_spec,
        out_shape=[jax.ShapeDtypeStruct((nseq, DEC_ROWS, A_WIDTH), F32),
                   jax.ShapeDtypeStruct((nseq, DEC_ROWS, B_WIDTH), F32)],
        compiler_params=_params("arbitrary"),
        name="decode_attend",
    )(page_table, *per_seq, lam_qk, g_sub, *caches)


def _rope_tables(pos):
    def cs(dim):
        half = dim // 2
        inv = ROPE_THETA ** (-jnp.arange(half, dtype=F32) / half)
        ang = pos.astype(F32)[:, None] * inv[None, :]
        return jnp.cos(ang), jnp.sin(ang)
    cos, sin = cs(HEAD_DIM)
    c128 = jnp.concatenate([cos, cos], axis=1)
    s128 = jnp.concatenate([-sin, sin], axis=1)
    cos, sin = cs(IDX_DIM)
    zero = jnp.zeros_like(sin)
    c64 = jnp.concatenate([cos, cos, cos, cos], axis=1)
    lo64 = jnp.concatenate([-sin, zero, -sin, zero], axis=1)
    hi64 = jnp.concatenate([zero, sin, zero, sin], axis=1)
    n = pos.shape[0]
    tail = jnp.concatenate([jnp.full((n, IDX_HEADS), IDX_HEADS ** -0.5, F32),
                            jnp.zeros((n, LANES - IDX_DIM - IDX_HEADS), F32)], axis=1)
    ck = jnp.concatenate([cos, cos, tail], axis=1)
    zero_tail = jnp.zeros((n, LANES - IDX_DIM), F32)
    klo = jnp.concatenate([-sin, zero, zero_tail], axis=1)
    khi = jnp.concatenate([zero, sin, zero_tail], axis=1)
    return (c128, s128), (c64, lo64, hi64, ck, klo, khi)


def _split_weights(w):
    sizes = (A_WIDTH, A_KV * HEAD_DIM, A_KV * HEAD_DIM, A_WIDTH, IDX_HEADS * IDX_DIM, IDX_DIM, IDX_HEADS,
             B_HEADS * 2 * HEAD_DIM, B_KV * 2 * HEAD_DIM, B_KV * B_VDIM, B_WIDTH, w.shape[0], w.shape[0])
    pts = np.cumsum(sizes)[:-1]
    q_a, k_a, v_a, z_a, qi, ki, wi, q_b, k_b, v_b, z_b, g_a, g_b = jnp.split(w, pts, axis=1)
    pad = jnp.zeros((w.shape[0], LANES - IDX_DIM - IDX_HEADS), w.dtype)
    cat = lambda parts: jnp.concatenate(parts, axis=1).astype(BF16)
    return (cat([q_a, q_b]), cat([k_b, v_b, k_a, v_a]), cat([qi, ki, wi, pad]), cat([z_a, z_b, g_a, g_b]))


def _project_all(h, weights, tabs, nb, lp, n_keep, expand):
    w_q, w_kv, w_idx, w_zg = weights
    qrot = _proj_q(h, w_q, tabs[0])
    kv = _proj_kv(h.reshape(nb, lp, h.shape[1]), w_kv, tabs[0], n_keep)
    idx = _proj_idx(h, w_idx, tabs[1], expand)
    zg = _proj_plain(h, w_zg, GATE_COL_TILE)
    return qrot, kv, idx, zg


def kernel(x_prompt, x_sample, cache_k_a, cache_v_a, cache_k_idx, cache_k_b, cache_v_b, page_table,
           meta_tokens, norm_g, w_in, lam_qk, subln_g, w_proj_a, w_proj_b, w_out, final_norm_g):
    nb, seq, d = x_prompt.shape
    nseq, n_new, _ = x_sample.shape
    depth = norm_g.shape[0]
    n_meta = meta_tokens.shape[0]
    n_pages = page_table.shape[1]
    page = cache_k_a.shape[2]
    past_len = n_pages * page
    n_tok = n_meta + seq
    lp = -(-n_tok // ROW_TILE) * ROW_TILE
    k_top_p = min(TOP_K_MAX, seq // 4)
    k_top_s = min(TOP_K_MAX, (past_len + n_new) // 4)

    meta = jnp.broadcast_to(meta_tokens.astype(x_prompt.dtype)[None], (nb, n_meta, d))
    xp = jnp.concatenate([meta, x_prompt, jnp.zeros((nb, lp - n_tok, d), x_prompt.dtype)], axis=1)
    xp = xp.reshape(nb * lp, d)
    xs = x_sample.reshape(nseq * n_new, d)
    tabs_p = _rope_tables(jnp.arange(lp, dtype=I32))
    tabs_s = _rope_tables(past_len + jnp.arange(nseq * n_new, dtype=I32) % n_new)

    np_, pg = cache_k_a.shape[1], cache_k_a.shape[2]
    caches = (jnp.swapaxes(cache_k_idx, 2, 3),
              cache_k_a.reshape(depth, np_, pg * A_KV, HEAD_DIM),
              cache_v_a.reshape(depth, np_, pg * A_KV, HEAD_DIM),
              cache_k_b.reshape(depth, np_, pg * B_KV * 2, HEAD_DIM),
              cache_v_b.reshape(depth, np_, pg, B_KV, 2, HEAD_DIM).transpose(0, 1, 2, 4, 3, 5)
              .reshape(depth, np_, pg * B_KV * 2, HEAD_DIM))

    hp = _rmsnorm(xp, norm_g[0], BF16)
    hs = _rmsnorm(xs, norm_g[0], BF16)
    rows_p = [[] for _ in range(5)]
    rows_s = [[] for _ in range(5)]
    pad_t = lambda a, axis: jnp.pad(a, [(0, DEC_ROWS - n_new) if ax == axis else (0, 0) for ax in range(a.ndim)])
    for l in range(depth):
        lam_init = _lambda_init(l)
        weights = _split_weights(w_in[l])
        w_pa, w_pb, w_o = w_proj_a[l].astype(BF16), w_proj_b[l].astype(BF16), w_out[l].astype(BF16)
        last = l + 1 == depth
        g_next = final_norm_g if last else norm_g[l + 1]
        norm_dtype = F32 if last else BF16
        lq = lam_qk[l].astype(F32)
        g_sub = subln_g[l].reshape(1, B_VDIM).astype(F32)

        qrot, (kvb, nkb, nvb, nka, nva), (qe, kw, kk), zg = _project_all(hp, weights, tabs_p, nb, lp, n_tok, True)
        o_a = _dsa_prompt(qrot, qe, kw, kk, kvb, nb, lp, k_top_p)
        o_b = _diff_prompt(qrot, kvb, lq, g_sub, nb, lp, lam_init)
        xp, hp = _merge(xp, o_a, o_b, zg, w_pa, w_pb, w_o, g_next, norm_dtype)
        ki_p = kw.reshape(nb, lp, LANES)[:, :n_tok, :IDX_DIM]
        for i, a in enumerate((nka.reshape(nb, n_tok, A_KV, HEAD_DIM), nva.reshape(nb, n_tok, A_KV, HEAD_DIM), ki_p,
                               nkb.reshape(nb, n_tok, B_KV, 2, HEAD_DIM), nvb.reshape(nb, n_tok, B_KV, B_VDIM))):
            rows_p[i].append(a)

        ms = nseq * n_new
        qrot, (_, nkb, nvb, nka, nva), (qi, kw, _), zg = _project_all(hs, weights, tabs_s, 1, ms, ms, False)
        qa8 = pad_t(qrot[:, :A_WIDTH].reshape(nseq, n_new, A_KV, A_GROUP, HEAD_DIM).transpose(0, 2, 3, 1, 4), 3)
        qa8 = qa8.reshape(nseq, A_KV, A_GROUP * DEC_ROWS, HEAD_DIM)
        qb8 = qrot[:, A_WIDTH:].reshape(nseq, n_new, B_KV, B_GROUP, 2, HEAD_DIM).transpose(0, 2, 4, 3, 1, 5)
        qb8 = pad_t(qb8, 4).reshape(nseq, B_KV, 2, B_GROUP * DEC_ROWS, HEAD_DIM)
        qi8 = pad_t(qi.reshape(nseq, n_new, IDX_HEADS, IDX_DIM).transpose(0, 2, 1, 3), 2)
        qi8 = qi8.reshape(nseq, IDX_HEADS * DEC_ROWS, IDX_DIM)
        w8 = pad_t(kw[:, IDX_DIM:IDX_DIM + IDX_HEADS].reshape(nseq, n_new, IDX_HEADS), 1)
        ki_s = kw[:, :IDX_DIM].reshape(nseq, n_new, IDX_DIM)
        new_s = (nka.reshape(nseq, n_new, A_KV, HEAD_DIM), nva.reshape(nseq, n_new, A_KV, HEAD_DIM), ki_s,
                 nkb.reshape(nseq, n_new, B_KV, 2, HEAD_DIM), nvb.reshape(nseq, n_new, B_KV, B_VDIM))
        new_rows = [pad_t(new_s[j], 1) for j in (2, 0, 1, 3, 4)]
        new_rows[1:4] = [a.reshape(nseq, -1, HEAD_DIM) for a in new_rows[1:4]]
        new_rows[0] = jnp.pad(jnp.swapaxes(ki_s, 1, 2), ((0, 0), (0, 0), (0, pg - n_new)))
        new_rows[4] = (new_rows[4].reshape(nseq, DEC_ROWS, B_KV, 2, HEAD_DIM).transpose(0, 1, 3, 2, 4)
                       .reshape(nseq, -1, HEAD_DIM))
        o_a8, o_b8 = _decode_attend(page_table, qi8, w8, qa8, qb8, new_rows, lq, g_sub, caches, l, past_len,
                                    n_new, k_top_s, lam_init)
        o_a = o_a8[:, :n_new].reshape(ms, A_WIDTH)
        o_b = o_b8[:, :n_new].reshape(ms, B_WIDTH)
        xs, hs = _merge(xs, o_a, o_b, zg, w_pa, w_pb, w_o, g_next, norm_dtype)
        for i, a in enumerate(new_s):
            rows_s[i].append(a)

    y_prompt = hp.reshape(nb, lp, d)[:, n_meta:n_tok]
    y_sample = hs.reshape(nseq, n_new, d)
    return (y_prompt, y_sample) + tuple(jnp.stack(r) for r in rows_p) + tuple(jnp.stack(r) for r in rows_s)
```
